```python
import jax, jax.numpy as jnp
from jax import lax
import numpy as np

D_MODEL = 2048
BATCH = 1
SEQ = 16384
DEPTH = 1

CHUNK = 64
N_LEFT_CHUNKS = 8
BAND = (N_LEFT_CHUNKS + 1) * CHUNK
CONV_CH = D_MODEL
CONV_WIDTH = 31
N_HEADS = 16
HEAD_DIM = 128
ATTN_W = N_HEADS * HEAD_DIM
MAX_REL = 128
N_REL = (CHUNK - 1) + MAX_REL + 1
PEER_HEADS = 8
PEER_KEYS = 128
N_EXPERTS = PEER_KEYS * PEER_KEYS
PEER_QDIM = 256
PEER_HALF = PEER_QDIM // 2
PEER_TOPK = 16
TOKEN_BLOCK = 128
EPS = 1e-6
IN_COLS = 2 * CONV_CH + 3 * ATTN_W + 2 * D_MODEL

kernel_name = "hybrid_conv_chunkattn_peer_block"


def rms_norm(x, g):
    xf = x.astype(jnp.float32)
    y = xf * lax.rsqrt(jnp.mean(xf * xf, axis=-1, keepdims=True) + EPS)
    return (y * g.astype(jnp.float32)).astype(x.dtype)


def layer_norm(x, g, b):
    xf = x.astype(jnp.float32)
    mu = jnp.mean(xf, axis=-1, keepdims=True)
    var = jnp.mean(jnp.square(xf - mu), axis=-1, keepdims=True)
    y = (xf - mu) * lax.rsqrt(var + EPS)
    return (y * g.astype(jnp.float32) + b.astype(jnp.float32)).astype(x.dtype)


def causal_depthwise_conv(x, w, b):
    y = lax.conv_general_dilated(
        x, w[:, None, :], window_strides=(1,), padding=[(CONV_WIDTH - 1, 0)],
        dimension_numbers=('NWC', 'WIO', 'NWC'), feature_group_count=x.shape[-1])
    return y + b


def conformer_conv_branch(a_val, a_gate, dw_w, dw_b, ln_g, ln_b, w_pw2):
    u = a_val * jax.nn.sigmoid(a_gate)
    u = causal_depthwise_conv(u, dw_w, dw_b)
    u = layer_norm(u, ln_g, ln_b)
    u = jax.nn.silu(u)
    return u @ w_pw2


def chunk_band_attention(q, k, v, rel_bias):
    B, S = q.shape[0], q.shape[1]
    n_chunks = S // CHUNK
    left = BAND - CHUNK
    k_pad = jnp.pad(k, ((0, 0), (left, 0), (0, 0), (0, 0)))
    v_pad = jnp.pad(v, ((0, 0), (left, 0), (0, 0), (0, 0)))
    qi = np.arange(CHUNK)[:, None]
    kj = np.arange(BAND)[None, :]
    rel_idx = np.clip(qi - kj + left, -(CHUNK - 1), MAX_REL) + (CHUNK - 1)
    bias = rel_bias[:, rel_idx].astype(jnp.float32)
    scale = HEAD_DIM ** -0.5

    def one_chunk(c):
        start = c * CHUNK
        qc = lax.dynamic_slice_in_dim(q, start, CHUNK, axis=1)
        kc = lax.dynamic_slice_in_dim(k_pad, start, BAND, axis=1)
        vc = lax.dynamic_slice_in_dim(v_pad, start, BAND, axis=1)
        s = jnp.einsum('bqhd,bkhd->bhqk', qc, kc).astype(jnp.float32) * scale + bias[None]
        kpos = start - left + jnp.arange(BAND, dtype=jnp.int32)
        s = jnp.where((kpos >= 0)[None, None, None, :], s, -1e30)
        p = jax.nn.softmax(s, axis=-1).astype(vc.dtype)
        return jnp.einsum('bhqk,bkhd->bqhd', p, vc)

    out = lax.map(one_chunk, jnp.arange(n_chunks, dtype=jnp.int32))
    return jnp.moveaxis(out, 0, 1).reshape(B, S, ATTN_W)


def peer_route(q, k1, k2):
    T = q.shape[0]
    s1 = jnp.einsum('thd,nd->thn', q[..., :PEER_HALF], k1).astype(jnp.float32)
    s2 = jnp.einsum('thd,nd->thn', q[..., PEER_HALF:], k2).astype(jnp.float32)
    v1, i1 = lax.top_k(s1, PEER_TOPK)
    v2, i2 = lax.top_k(s2, PEER_TOPK)
    cand = (v1[..., :, None] + v2[..., None, :]).reshape(T, PEER_HEADS, PEER_TOPK * PEER_TOPK)
    top, flat = lax.top_k(cand, PEER_TOPK)
    e1 = jnp.take_along_axis(i1, flat // PEER_TOPK, axis=-1)
    e2 = jnp.take_along_axis(i2, flat % PEER_TOPK, axis=-1)
    idx = e1 * PEER_KEYS + e2
    gate = jax.nn.softmax(top, axis=-1)
    return idx, gate


def peer_ffn(h, w_query, k1, k2, expert_u, expert_v):
    B, S, D = h.shape
    T = B * S
    ht = h.reshape(T, D)
    q = (ht @ w_query).reshape(T, PEER_HEADS, PEER_QDIM)
    idx, gate = peer_route(q, k1, k2)
    nb = T // TOKEN_BLOCK

    def block(args):
        hb, ib, gb = args
        u = expert_u[ib]
        a = jnp.einsum('td,thkd->thk', hb, u)
        w = gb.astype(hb.dtype) * jax.nn.gelu(a, approximate=False)
        v = expert_v[ib]
        return jnp.einsum('thk,thkd->td', w, v)

    y = lax.map(block, (ht.reshape(nb, TOKEN_BLOCK, D),
                        idx.reshape(nb, TOKEN_BLOCK, PEER_HEADS, PEER_TOPK),
                        gate.reshape(nb, TOKEN_BLOCK, PEER_HEADS, PEER_TOPK)))
    return y.reshape(B, S, D)


def setup_inputs(seed: int = 0) -> dict:
    key = jax.random.key(seed)
    ks = jax.random.split(key, 20)
    L, D, C, A = DEPTH, D_MODEL, CONV_CH, ATTN_W
    nrm = lambda k, shape, s: jax.random.normal(k, shape, jnp.float32) * s
    return {
        "x": nrm(ks[0], (BATCH, SEQ, D), 1.0),
        "norm1_g": 1.0 + nrm(ks[1], (L, D), 0.02),
        "w_in": nrm(ks[2], (L, D, IN_COLS), D ** -0.5),
        "conv_dw_w": nrm(ks[3], (L, CONV_WIDTH, C), CONV_WIDTH ** -0.5),
        "conv_dw_b": nrm(ks[4], (L, C), 0.02),
        "conv_ln_g": 1.0 + nrm(ks[5], (L, C), 0.02),
        "conv_ln_b": nrm(ks[6], (L, C), 0.02),
        "w_conv_out": nrm(ks[7], (L, C, D), C ** -0.5),
        "q_norm_g": 1.0 + nrm(ks[8], (L, HEAD_DIM), 0.02),
        "k_norm_g": 1.0 + nrm(ks[9], (L, HEAD_DIM), 0.02),
        "rel_bias": nrm(ks[10], (L, N_HEADS, N_REL), 0.1),
        "w_attn_o": nrm(ks[11], (L, A, D), A ** -0.5),
        "w_out": nrm(ks[12], (L, D, D), D ** -0.5),
        "norm2_g": 1.0 + nrm(ks[13], (L, D), 0.02),
        "w_query": nrm(ks[14], (L, D, PEER_HEADS * PEER_QDIM), D ** -0.5),
        "sub_keys_1": nrm(ks[15], (L, PEER_KEYS, PEER_HALF), PEER_HALF ** -0.5),
        "sub_keys_2": nrm(ks[16], (L, PEER_KEYS, PEER_HALF), PEER_HALF ** -0.5),
        "expert_u": nrm(ks[17], (L, N_EXPERTS, D), D ** -0.5),
        "expert_v": nrm(ks[18], (L, N_EXPERTS, D), PEER_HEADS ** -0.5),
    }


def reference(x, norm1_g, w_in, conv_dw_w, conv_dw_b, conv_ln_g, conv_ln_b, w_conv_out,
              q_norm_g, k_norm_g, rel_bias, w_attn_o, w_out, norm2_g, w_query,
              sub_keys_1, sub_keys_2, expert_u, expert_v):
    B, S, D = x.shape
    splits = np.cumsum([CONV_CH, CONV_CH, ATTN_W, ATTN_W, ATTN_W, D_MODEL]).tolist()
    for l in range(DEPTH):
        h = rms_norm(x, norm1_g[l])
        proj = h @ w_in[l]
        a_val, a_gate, q, k, v, g_conv, g_attn = jnp.split(proj, splits, axis=-1)
        conv_out = conformer_conv_branch(a_val, a_gate, conv_dw_w[l], conv_dw_b[l],
                                         conv_ln_g[l], conv_ln_b[l], w_conv_out[l])
        q = rms_norm(q.reshape(B, S, N_HEADS, HEAD_DIM), q_norm_g[l])
        k = rms_norm(k.reshape(B, S, N_HEADS, HEAD_DIM), k_norm_g[l])
        v = v.reshape(B, S, N_HEADS, HEAD_DIM)
        attn_out = chunk_band_attention(q, k, v, rel_bias[l]) @ w_attn_o[l]
        merged = jax.nn.sigmoid(g_conv) * conv_out + jax.nn.sigmoid(g_attn) * attn_out
        x = x + merged @ w_out[l]
        h2 = rms_norm(x, norm2_g[l])
        x = x + peer_ffn(h2, w_query[l], sub_keys_1[l], sub_keys_2[l], expert_u[l], expert_v[l])
    return x
```

```python
import functools

import numpy as np
import jax
import jax.numpy as jnp
from jax import lax
from jax.experimental import pallas as pl
from jax.experimental.pallas import tpu as pltpu

F32 = jnp.float32
BF16 = jnp.bfloat16

EPS = 1e-6
CHUNK = 64
N_LEFT_CHUNKS = 8
CONV_WIDTH = 31
N_HEADS = 16
HEAD_DIM = 128
MAX_REL = 128
PEER_HEADS = 8
PEER_KEYS = 128
PEER_HALF = 128
PEER_TOPK = 16
MASK_VALUE = -1e30

VMEM_LIMIT_BYTES = 56 * 1024 * 1024
CONV_HALO = 32
ATTN_QBLOCK = 4 * CHUNK
ATTN_KBLOCKS = 1 + (N_LEFT_CHUNKS * CHUNK) // ATTN_QBLOCK


def _params(sem):
    return pltpu.CompilerParams(dimension_semantics=sem, vmem_limit_bytes=VMEM_LIMIT_BYTES)


def _sigmoid(x):
    return 1.0 / (1.0 + jnp.exp(-x))


def _inproj_kernel(x_ref, g_ref, w_ref, o_ref, h_scr):
    @pl.when(pl.program_id(1) == 0)
    def _():
        x = x_ref[...]
        ms = jnp.mean(x * x, axis=-1, keepdims=True)
        h_scr[...] = (x * lax.rsqrt(ms + EPS) * g_ref[...]).astype(BF16)

    o_ref[...] = jnp.dot(h_scr[...], w_ref[...], preferred_element_type=F32).astype(o_ref.dtype)


def _in_proj(x, g, w, tm, tn):
    T, D = x.shape
    N = w.shape[1]
    return pl.pallas_call(
        _inproj_kernel,
        grid=(T // tm, N // tn),
        in_specs=[
            pl.BlockSpec((tm, D), lambda i, j: (i, 0)),
            pl.BlockSpec((1, D), lambda i, j: (0, 0)),
            pl.BlockSpec((D, tn), lambda i, j: (0, j)),
        ],
        out_specs=pl.BlockSpec((tm, tn), lambda i, j: (i, j)),
        out_shape=jax.ShapeDtypeStruct((T, N), BF16),
        scratch_shapes=[pltpu.VMEM((tm, D), BF16)],
        compiler_params=_params(("parallel", "arbitrary")),
        name="in_proj",
    )(x, g, w)


CONV_ROWS = 64
CONV_COLS = 256


def _conv_kernel(val_ref, gate_ref, hval_ref, hgate_ref, gc_ref, dww_ref, dwb_ref,
                 lng_ref, lnb_ref, wpw_ref, o_ref, u_scr, acc_scr):
    i = pl.program_id(0)
    ts, C = val_ref.shape
    halo = hval_ref.shape[0]
    hu = hval_ref[...].astype(F32) * _sigmoid(hgate_ref[...].astype(F32))
    u_scr[0:halo, :] = jnp.where(i > 0, hu, 0.0)
    u_scr[halo:halo + ts, :] = val_ref[...].astype(F32) * _sigmoid(gate_ref[...].astype(F32))

    off = halo - (CONV_WIDTH - 1)
    for r0 in range(0, ts, CONV_ROWS):
        for c0 in range(0, C, CONV_COLS):
            acc = jnp.broadcast_to(dwb_ref[:, c0:c0 + CONV_COLS], (CONV_ROWS, CONV_COLS))
            for j in range(CONV_WIDTH):
                acc = acc + dww_ref[j:j + 1, c0:c0 + CONV_COLS] * \
                    u_scr[off + j + r0:off + j + r0 + CONV_ROWS, c0:c0 + CONV_COLS]
            acc_scr[r0:r0 + CONV_ROWS, c0:c0 + CONV_COLS] = acc

    y = acc_scr[...]
    mu = jnp.mean(y, axis=-1, keepdims=True)
    yc = y - mu
    var = jnp.mean(yc * yc, axis=-1, keepdims=True)
    z = yc * lax.rsqrt(var + EPS) * lng_ref[...] + lnb_ref[...]
    z = z * _sigmoid(z)
    out = jnp.dot(z.astype(BF16), wpw_ref[...], preferred_element_type=F32)
    o_ref[...] = (out * _sigmoid(gc_ref[...].astype(F32))).astype(o_ref.dtype)


def _conv_branch(proj, dww, dwb, lng, lnb, wpw, C, ts, gconv_col_block):
    T = proj.shape[0]
    hb = ts // CONV_HALO
    halo_map_v = lambda i: (jnp.maximum(i * hb - 1, 0), 0)
    halo_map_g = lambda i: (jnp.maximum(i * hb - 1, 0), 1)
    const = lambda i: (0, 0)
    return pl.pallas_call(
        _conv_kernel,
        grid=(T // ts,),
        in_specs=[
            pl.BlockSpec((ts, C), lambda i: (i, 0)),
            pl.BlockSpec((ts, C), lambda i: (i, 1)),
            pl.BlockSpec((CONV_HALO, C), halo_map_v),
            pl.BlockSpec((CONV_HALO, C), halo_map_g),
            pl.BlockSpec((ts, C), lambda i: (i, gconv_col_block)),
            pl.BlockSpec((CONV_WIDTH, C), const),
            pl.BlockSpec((1, C), const),
            pl.BlockSpec((1, C), const),
            pl.BlockSpec((1, C), const),
            pl.BlockSpec((C, C), const),
        ],
        out_specs=pl.BlockSpec((ts, C), lambda i: (i, 0)),
        out_shape=jax.ShapeDtypeStruct((T, C), BF16),
        scratch_shapes=[pltpu.VMEM((CONV_HALO + ts, C), F32), pltpu.VMEM((ts, C), F32)],
        compiler_params=_params(("parallel",)),
        name="conv_branch",
    )(proj, proj, proj, proj, proj, dww, dwb, lng, lnb, wpw)


def _attn_kernel(q_ref, *rest):
    k_refs = rest[:ATTN_KBLOCKS]
    v_refs = rest[ATTN_KBLOCKS:2 * ATTN_KBLOCKS]
    bias_ref, qg_ref, kg_ref, o_ref = rest[2 * ATTN_KBLOCKS:]
    i = pl.program_id(1)
    qb = q_ref.shape[0]
    nkb = ATTN_KBLOCKS

    def rms(t, g):
        t = t.astype(F32)
        return t * lax.rsqrt(jnp.mean(t * t, axis=-1, keepdims=True) + EPS) * g

    q = (rms(q_ref[...], qg_ref[...]) * (HEAD_DIM ** -0.5)).astype(BF16)
    k = jnp.concatenate([rms(r[...], kg_ref[...]).astype(BF16) for r in k_refs], axis=0)
    v = jnp.concatenate([r[...] for r in v_refs], axis=0)
    s = lax.dot_general(q, k, (((1,), (1,)), ((), ())), preferred_element_type=F32)
    s = s + bias_ref[0]
    col = lax.broadcasted_iota(jnp.int32, s.shape, 1)
    s = jnp.where(col >= (nkb - 1 - i) * qb, s, MASK_VALUE)
    m = jnp.max(s, axis=-1, keepdims=True)
    p = jnp.exp(s - m)
    l = jnp.sum(p, axis=-1, keepdims=True)
    o = jnp.dot(p.astype(BF16), v, preferred_element_type=F32)
    o_ref[...] = (o / l).astype(o_ref.dtype)


def _attn_bias_table(rel_bias):
    qb, kb = ATTN_QBLOCK, ATTN_KBLOCKS * ATTN_QBLOCK
    qpos = np.arange(qb)[:, None]
    kpos = np.arange(kb)[None, :] - (kb - qb)
    qc = qpos // CHUNK
    kc = np.floor_divide(kpos, CHUNK)
    valid = (kc <= qc) & (kc >= qc - N_LEFT_CHUNKS)
    idx = np.clip(qpos - kpos, -(CHUNK - 1), MAX_REL) + (CHUNK - 1)
    bias = rel_bias[:, idx].astype(F32)
    return jnp.where(valid[None], bias, MASK_VALUE)


def _attention(proj, bias, qg, kg, q_col0, k_col0, v_col0):
    T = proj.shape[0]
    qb = ATTN_QBLOCK
    nkb = ATTN_KBLOCKS
    hd = HEAD_DIM

    def kv_spec(col0, back):
        return pl.BlockSpec((qb, hd), lambda h, i: (jnp.maximum(i - back, 0), col0 + h))

    k_specs = [kv_spec(k_col0, nkb - 1 - b) for b in range(nkb)]
    v_specs = [kv_spec(v_col0, nkb - 1 - b) for b in range(nkb)]
    return pl.pallas_call(
        _attn_kernel,
        grid=(N_HEADS, T // qb),
        in_specs=[pl.BlockSpec((qb, hd), lambda h, i: (i, q_col0 + h))] + k_specs + v_specs + [
            pl.BlockSpec((1, qb, nkb * qb), lambda h, i: (h, 0, 0)),
            pl.BlockSpec((1, hd), lambda h, i: (0, 0)),
            pl.BlockSpec((1, hd), lambda h, i: (0, 0)),
        ],
        out_specs=pl.BlockSpec((qb, hd), lambda h, i: (i, h)),
        out_shape=jax.ShapeDtypeStruct((T, N_HEADS * hd), BF16),
        compiler_params=_params(("parallel", "arbitrary")),
        name="chunk_attention",
    )(*([proj] * (1 + 2 * nkb)), bias, qg, kg)


def _merge_kernel(conv_ref, attn_ref, ga_ref, x_ref, wao_ref, wout_ref, g2_ref, wq_ref,
                  x1_ref, h2_ref, qp_ref):
    a = jnp.dot(attn_ref[...], wao_ref[...], preferred_element_type=F32)
    merged = conv_ref[...].astype(F32) + _sigmoid(ga_ref[...].astype(F32)) * a
    x1 = x_ref[...] + jnp.dot(merged.astype(BF16), wout_ref[...], preferred_element_type=F32)
    x1_ref[...] = x1
    ms = jnp.mean(x1 * x1, axis=-1, keepdims=True)
    h2 = (x1 * lax.rsqrt(ms + EPS) * g2_ref[...]).astype(BF16)
    h2_ref[...] = h2
    qp_ref[...] = jnp.dot(h2, wq_ref[...], preferred_element_type=F32).astype(qp_ref.dtype)


def _merge(convg, attn, proj, x, wao, wout, g2, wq, tm, gattn_col_block):
    T, D = x.shape
    row = lambda i: (i, 0)
    const = lambda i: (0, 0)
    resident = functools.partial(pl.BlockSpec, pipeline_mode=pl.Buffered(1))
    return pl.pallas_call(
        _merge_kernel,
        grid=(T // tm,),
        in_specs=[
            pl.BlockSpec((tm, D), row),
            pl.BlockSpec((tm, D), row),
            pl.BlockSpec((tm, D), lambda i: (i, gattn_col_block)),
            pl.BlockSpec((tm, D), row),
            resident(wao.shape, const),
            resident(wout.shape, const),
            pl.BlockSpec((1, D), const),
            resident(wq.shape, const),
        ],
        out_specs=[pl.BlockSpec((tm, D), row), pl.BlockSpec((tm, D), row),
                   pl.BlockSpec((tm, wq.shape[1]), row)],
        out_shape=[jax.ShapeDtypeStruct((T, D), F32), jax.ShapeDtypeStruct((T, D), BF16),
                   jax.ShapeDtypeStruct((T, wq.shape[1]), BF16)],
        compiler_params=_params(("parallel",)),
        name="merge_outproj",
    )(convg, attn, proj, x, wao, wout, g2, wq)


_CAND_ROWS = (16, 8, 8, 8, 8, 8, 8, 8)


def _distinct_top(s, n):
    vals, cnts = [], []
    for _ in range(n):
        m = jnp.max(s, axis=0, keepdims=True)
        eq = s == m
        vals.append(m)
        cnts.append(jnp.sum(jnp.where(eq, 1.0, 0.0), axis=0, keepdims=True))
        s = jnp.where(eq, -jnp.inf, s)
    return vals, cnts


def _route_kernel(q_ref, k1_ref, k2_ref, cut_ref, r1_ref, s2_ref, e2_ref):
    nt = (((1,), (1,)), ((), ()))
    for h in range(PEER_HEADS):
        base = h * 2 * PEER_HALF
        s1 = lax.dot_general(k1_ref[...], q_ref[:, base:base + PEER_HALF], nt,
                             preferred_element_type=F32)
        s2 = lax.dot_general(k2_ref[...], q_ref[:, base + PEER_HALF:base + 2 * PEER_HALF], nt,
                             preferred_element_type=F32)
        v1, c1 = _distinct_top(s1, PEER_TOPK)
        v2, c2 = _distinct_top(s2, PEER_TOPK)
        V1, C1 = jnp.concatenate(v1, axis=0), jnp.concatenate(c1, axis=0)
        V2, C2 = jnp.concatenate(v2, axis=0), jnp.concatenate(c2, axis=0)
        cands, wgts = [], []
        for i, n in enumerate(_CAND_ROWS):
            cands.append(v1[i] + V2[:n])
            wgts.append(c1[i] * C2[:n])
        cands.append(V1[8:] + v2[0])
        wgts.append(C1[8:] * c2[0])
        cand = jnp.concatenate(cands, axis=0)
        wgt = jnp.concatenate(wgts, axis=0)

        top = v1[0] + v2[0]
        tau = jnp.full_like(top, -jnp.inf)
        cum = jnp.zeros_like(top)
        work = cand
        for _ in range(PEER_TOPK):
            m = jnp.max(work, axis=0, keepdims=True)
            eq = work == m
            new_cum = cum + jnp.sum(jnp.where(eq, wgt, 0.0), axis=0, keepdims=True)
            tau = jnp.where((cum < PEER_TOPK) & (new_cum >= PEER_TOPK), m, tau)
            cum = new_cum
            work = jnp.where(eq, -jnp.inf, work)
        z = jnp.sum(jnp.where(cand >= tau, wgt * jnp.exp(cand - top), 0.0), axis=0, keepdims=True)

        cut = jnp.full_like(s1, jnp.inf)
        for j in range(PEER_TOPK):
            cut = jnp.where(s1 + v2[j] >= tau, v2[j], cut)
        cut_ref[h] = cut
        r1_ref[h] = jnp.exp(s1 - v1[0])
        s2_ref[h] = s2
        e2_ref[h] = jnp.exp(s2 - v2[0]) / z


def _route(qp, k1, k2, tr):
    T = qp.shape[0]
    out = jax.ShapeDtypeStruct((PEER_HEADS, PEER_KEYS, T), F32)
    ospec = pl.BlockSpec((PEER_HEADS, PEER_KEYS, tr), lambda i: (0, 0, i))
    return pl.pallas_call(
        _route_kernel,
        grid=(T // tr,),
        in_specs=[
            pl.BlockSpec((tr, qp.shape[1]), lambda i: (i, 0)),
            pl.BlockSpec(k1.shape, lambda i: (0, 0)),
            pl.BlockSpec(k2.shape, lambda i: (0, 0)),
        ],
        out_specs=[ospec] * 4,
        out_shape=[out] * 4,
        compiler_params=_params(("parallel",)),
        name="peer_route",
    )(qp, k1, k2)


def _gelu(a):
    return 0.5 * a * (1.0 + lax.erf(a * (2.0 ** -0.5)))


def _peer_kernel(h2_ref, u_ref, vt_ref, cut_ref, r1_ref, s2_ref, e2_ref, x1_ref, o_ref,
                 y_scr, p_scr):
    j = pl.program_id(1)
    te = u_ref.shape[0]
    n_e1 = te // PEER_KEYS

    @pl.when(j == 0)
    def _():
        y_scr[...] = jnp.zeros_like(y_scr)

    a = lax.dot_general(u_ref[...], h2_ref[...], (((1,), (1,)), ((), ())),
                        preferred_element_type=F32)
    for r in range(n_e1):
        e1 = j * n_e1 + r
        w = None
        for h in range(PEER_HEADS):
            cut = cut_ref[h, pl.ds(e1, 1), :]
            r1 = r1_ref[h, pl.ds(e1, 1), :]
            t = jnp.where(s2_ref[h] >= cut, e2_ref[h], 0.0) * r1
            w = t if w is None else w + t
        ar = a[r * PEER_KEYS:(r + 1) * PEER_KEYS]
        p_scr[r * PEER_KEYS:(r + 1) * PEER_KEYS, :] = (_gelu(ar) * w).astype(BF16)
    y_scr[...] += jnp.dot(vt_ref[...], p_scr[...], preferred_element_type=F32)

    @pl.when(j == pl.num_programs(1) - 1)
    def _():
        o_ref[...] = x1_ref[...] + y_scr[...].T


def _peer(h2, u, vt, cut, r1, s2, e2, x1, tm, te):
    T, D = h2.shape
    E = u.shape[0]
    once = functools.partial(pl.BlockSpec, pipeline_mode=pl.Buffered(1))
    rspec = once((PEER_HEADS, PEER_KEYS, tm), lambda i, j: (0, 0, i))
    return pl.pallas_call(
        _peer_kernel,
        grid=(T // tm, E // te),
        in_specs=[
            once((tm, D), lambda i, j: (i, 0)),
            pl.BlockSpec((te, D), lambda i, j: (j, 0)),
            pl.BlockSpec((D, te), lambda i, j: (0, j)),
            rspec, rspec, rspec, rspec,
            once((tm, D), lambda i, j: (i, 0)),
        ],
        out_specs=pl.BlockSpec((tm, D), lambda i, j: (i, 0)),
        out_shape=jax.ShapeDtypeStruct((T, D), F32),
        scratch_shapes=[pltpu.VMEM((D, tm), F32), pltpu.VMEM((te, tm), BF16)],
        compiler_params=_params(("parallel", "arbitrary")),
        name="peer_experts",
    )(h2, u, vt, cut, r1, s2, e2, x1)


def kernel(x, norm1_g, w_in, conv_dw_w, conv_dw_b, conv_ln_g, conv_ln_b, w_conv_out, q_norm_g,
           k_norm_g, rel_bias, w_attn_o, w_out, norm2_g, w_query, sub_keys_1, sub_keys_2,
           expert_u, expert_v):
    B, S, D = x.shape
    assert B == 1, "sequence-causal kernels are written for a single sequence"
    T = S
    C = conv_dw_w.shape[-1]
    A = N_HEADS * HEAD_DIM
    assert C == D and A == D and T % 512 == 0
    xt = x.reshape(T, D)
    for l in range(norm1_g.shape[0]):
        row = lambda v: v[l].reshape(1, -1).astype(F32)
        proj = _in_proj(xt, row(norm1_g), w_in[l].astype(BF16), tm=min(1024, T), tn=1024)
        convg = _conv_branch(proj, conv_dw_w[l], row(conv_dw_b), row(conv_ln_g), row(conv_ln_b),
                             w_conv_out[l].astype(BF16), C, ts=256,
                             gconv_col_block=(2 * C + 3 * A) // C)
        attn = _attention(proj, _attn_bias_table(rel_bias[l]), row(q_norm_g), row(k_norm_g),
                          q_col0=2 * C // HEAD_DIM, k_col0=(2 * C + A) // HEAD_DIM,
                          v_col0=(2 * C + 2 * A) // HEAD_DIM)
        x1, h2, qp = _merge(convg, attn, proj, xt, w_attn_o[l].astype(BF16), w_out[l].astype(BF16),
                            row(norm2_g), w_query[l].astype(BF16), tm=256,
                            gattn_col_block=(2 * C + 3 * A + D) // D)
        cut, r1, s2, e2 = _route(qp, sub_keys_1[l].astype(BF16), sub_keys_2[l].astype(BF16), tr=256)
        xt = _peer(h2, expert_u[l].astype(BF16), expert_v[l].astype(BF16).T, cut, r1, s2, e2, x1,
                   tm=512, te=1024)
    return xt.reshape(B, S, D)
```

```python
import functools

import numpy as np
import jax
import jax.numpy as jnp
from jax import lax
from jax.experimental import pallas as pl
from jax.experimental.pallas import tpu as pltpu

F32 = jnp.float32
BF16 = jnp.bfloat16

EPS = 1e-6
CHUNK = 64
N_LEFT_CHUNKS = 8
CONV_WIDTH = 31
N_HEADS = 16
HEAD_DIM = 128
MAX_REL = 128
PEER_HEADS = 8
PEER_KEYS = 128
PEER_HALF = 128
PEER_TOPK = 16
MASK_VALUE = -1e30
LANES = 128
PEER_LANES = LANES

VMEM_LIMIT_BYTES = 56 * 1024 * 1024
CONV_HALO = 32
CONV_ROWS = 64
ATTN_QBLOCK = 4 * CHUNK
ATTN_KBLOCKS = 1 + (N_LEFT_CHUNKS * CHUNK) // ATTN_QBLOCK
ATTN_HEADS_PER_STEP = 4


def _params(sem):
    return pltpu.CompilerParams(dimension_semantics=sem, vmem_limit_bytes=VMEM_LIMIT_BYTES)


def _sigmoid(x):
    return 1.0 / (1.0 + jnp.exp(-x))


def _inproj_kernel(x_ref, g_ref, w_ref, o_ref, h_scr):
    @pl.when(pl.program_id(1) == 0)
    def _():
        x = x_ref[...]
        ms = jnp.mean(x * x, axis=-1, keepdims=True)
        h_scr[...] = (x * lax.rsqrt(ms + EPS) * g_ref[...]).astype(BF16)

    o_ref[...] = jnp.dot(h_scr[...], w_ref[...], preferred_element_type=F32).astype(o_ref.dtype)


def _in_proj(x, g, w, tm, tn):
    T, D = x.shape
    N = w.shape[1]
    return pl.pallas_call(
        _inproj_kernel,
        grid=(T // tm, N // tn),
        in_specs=[
            pl.BlockSpec((tm, D), lambda i, j: (i, 0)),
            pl.BlockSpec((1, D), lambda i, j: (0, 0)),
            pl.BlockSpec((D, tn), lambda i, j: (0, j)),
        ],
        out_specs=pl.BlockSpec((tm, tn), lambda i, j: (i, j)),
        out_shape=jax.ShapeDtypeStruct((T, N), BF16),
        scratch_shapes=[pltpu.VMEM((tm, D), BF16)],
        compiler_params=_params(("parallel", "arbitrary")),
        name="in_proj",
    )(x, g, w)


def _conv_kernel(val_ref, gate_ref, hval_ref, hgate_ref, gc_ref, dww_ref, dwb_ref,
                 lng_ref, lnb_ref, wpw_ref, o_ref, u_scr, y_scr):
    i = pl.program_id(0)
    ts, C = val_ref.shape
    halo = hval_ref.shape[0]
    ncb = C // LANES
    hu = hval_ref[...].astype(F32) * _sigmoid(hgate_ref[...].astype(F32))
    hu = jnp.where(i > 0, hu, 0.0)
    u = val_ref[...].astype(F32) * _sigmoid(gate_ref[...].astype(F32))
    for cb in range(ncb):
        lanes = slice(cb * LANES, (cb + 1) * LANES)
        u_scr[cb, 0:halo, :] = hu[:, lanes]
        u_scr[cb, halo:halo + ts, :] = u[:, lanes]

    off = halo - (CONV_WIDTH - 1)
    for cb in range(ncb):
        lanes = slice(cb * LANES, (cb + 1) * LANES)
        taps = [jnp.broadcast_to(dww_ref[j:j + 1, lanes], (8, LANES)) for j in range(CONV_WIDTH)]
        bias = jnp.broadcast_to(dwb_ref[:, lanes], (8, LANES))
        for rb in range(0, ts, CONV_ROWS):
            acc = [bias] * 8
            for s in range(8 + CONV_WIDTH - 1):
                xs = u_scr[cb, pl.ds(off + rb + s, 8, stride=8), :]
                for q in range(8):
                    if 0 <= s - q < CONV_WIDTH:
                        acc[q] = acc[q] + taps[s - q] * xs
            for q in range(8):
                y_scr[cb, pl.ds(rb + q, 8, stride=8), :] = acc[q]

    y = jnp.concatenate([y_scr[cb] for cb in range(ncb)], axis=1)
    mu = jnp.mean(y, axis=-1, keepdims=True)
    yc = y - mu
    var = jnp.mean(yc * yc, axis=-1, keepdims=True)
    z = yc * lax.rsqrt(var + EPS) * lng_ref[...] + lnb_ref[...]
    z = z * _sigmoid(z)
    out = jnp.dot(z.astype(BF16), wpw_ref[...], preferred_element_type=F32)
    o_ref[...] = (out * _sigmoid(gc_ref[...].astype(F32))).astype(o_ref.dtype)


def _conv_branch(proj, dww, dwb, lng, lnb, wpw, C, ts, gconv_col_block):
    T = proj.shape[0]
    assert ts % CONV_ROWS == 0 and C % LANES == 0
    hb = ts // CONV_HALO
    halo_map_v = lambda i: (jnp.maximum(i * hb - 1, 0), 0)
    halo_map_g = lambda i: (jnp.maximum(i * hb - 1, 0), 1)
    const = lambda i: (0, 0)
    return pl.pallas_call(
        _conv_kernel,
        grid=(T // ts,),
        in_specs=[
            pl.BlockSpec((ts, C), lambda i: (i, 0)),
            pl.BlockSpec((ts, C), lambda i: (i, 1)),
            pl.BlockSpec((CONV_HALO, C), halo_map_v),
            pl.BlockSpec((CONV_HALO, C), halo_map_g),
            pl.BlockSpec((ts, C), lambda i: (i, gconv_col_block)),
            pl.BlockSpec((CONV_WIDTH, C), const),
            pl.BlockSpec((1, C), const),
            pl.BlockSpec((1, C), const),
            pl.BlockSpec((1, C), const),
            pl.BlockSpec((C, C), const),
        ],
        out_specs=pl.BlockSpec((ts, C), lambda i: (i, 0)),
        out_shape=jax.ShapeDtypeStruct((T, C), BF16),
        scratch_shapes=[pltpu.VMEM((C // LANES, CONV_HALO + ts, LANES), F32),
                        pltpu.VMEM((C // LANES, ts, LANES), F32)],
        compiler_params=_params(("parallel",)),
        name="conv_branch",
    )(proj, proj, proj, proj, proj, dww, dwb, lng, lnb, wpw)


def _attn_kernel(q_ref, *rest):
    k_refs = rest[:ATTN_KBLOCKS]
    v_refs = rest[ATTN_KBLOCKS:2 * ATTN_KBLOCKS]
    bias_ref, qg_ref, kg_ref, o_ref = rest[2 * ATTN_KBLOCKS:]
    i = pl.program_id(1)
    qb = q_ref.shape[0]
    nkb = ATTN_KBLOCKS

    def rms(t, g):
        t = t.astype(F32)
        return t * lax.rsqrt(jnp.mean(t * t, axis=-1, keepdims=True) + EPS) * g

    col = lax.broadcasted_iota(jnp.int32, (1, nkb * qb), 1)
    in_seq = col >= (nkb - 1 - i) * qb
    for hh in range(ATTN_HEADS_PER_STEP):
        lanes = slice(hh * HEAD_DIM, (hh + 1) * HEAD_DIM)
        q = (rms(q_ref[:, lanes], qg_ref[...]) * (HEAD_DIM ** -0.5)).astype(BF16)
        k = jnp.concatenate([rms(r[:, lanes], kg_ref[...]).astype(BF16) for r in k_refs], axis=0)
        v = jnp.concatenate([r[:, lanes] for r in v_refs], axis=0)
        s = lax.dot_general(q, k, (((1,), (1,)), ((), ())), preferred_element_type=F32)
        s = jnp.where(in_seq, s + bias_ref[hh], MASK_VALUE)
        m = jnp.max(s, axis=-1, keepdims=True)
        p = jnp.exp(s - m)
        l = jnp.sum(p, axis=-1, keepdims=True)
        o = jnp.dot(p.astype(BF16), v, preferred_element_type=F32)
        o_ref[:, lanes] = (o / l).astype(o_ref.dtype)


def _attn_bias_table(rel_bias):
    qb, kb = ATTN_QBLOCK, ATTN_KBLOCKS * ATTN_QBLOCK
    n = 1 << int(np.ceil(np.log2(qb + kb)))
    dist = (kb - qb) - (np.arange(n) - (qb - 1))
    f = rel_bias[:, np.clip(dist, -(CHUNK - 1), MAX_REL) + (CHUNK - 1)].astype(F32)
    flat = jnp.tile(f, (1, qb))[:, :qb * (n - 1)]
    bias = flat.reshape(-1, qb, n - 1)[:, :, qb - 1:qb - 1 + kb]
    qpos = np.arange(qb)[:, None]
    kpos = np.arange(kb)[None, :] - (kb - qb)
    qc = qpos // CHUNK
    kc = np.floor_divide(kpos, CHUNK)
    valid = (kc <= qc) & (kc >= qc - N_LEFT_CHUNKS)
    return jnp.where(valid[None], bias, MASK_VALUE)


def _attention(proj, bias, qg, kg, q_col0, k_col0, v_col0):
    T = proj.shape[0]
    qb = ATTN_QBLOCK
    nkb = ATTN_KBLOCKS
    hp = ATTN_HEADS_PER_STEP
    w = hp * HEAD_DIM
    assert N_HEADS % hp == 0 and q_col0 % hp == 0 and k_col0 % hp == 0 and v_col0 % hp == 0

    def kv_spec(col0, back):
        return pl.BlockSpec((qb, w), lambda g, i: (jnp.maximum(i - back, 0), col0 // hp + g))

    k_specs = [kv_spec(k_col0, nkb - 1 - b) for b in range(nkb)]
    v_specs = [kv_spec(v_col0, nkb - 1 - b) for b in range(nkb)]
    return pl.pallas_call(
        _attn_kernel,
        grid=(N_HEADS // hp, T // qb),
        in_specs=[pl.BlockSpec((qb, w), lambda g, i: (i, q_col0 // hp + g))] + k_specs + v_specs + [
            pl.BlockSpec((hp, qb, nkb * qb), lambda g, i: (g, 0, 0)),
            pl.BlockSpec((1, HEAD_DIM), lambda g, i: (0, 0)),
            pl.BlockSpec((1, HEAD_DIM), lambda g, i: (0, 0)),
        ],
        out_specs=pl.BlockSpec((qb, w), lambda g, i: (i, g)),
        out_shape=jax.ShapeDtypeStruct((T, N_HEADS * HEAD_DIM), BF16),
        compiler_params=_params(("parallel", "arbitrary")),
        name="chunk_attention",
    )(*([proj] * (1 + 2 * nkb)), bias, qg, kg)


def _merge_kernel(conv_ref, attn_ref, ga_ref, x_ref, wao_ref, wout_ref, g2_ref, wq_ref,
                  x1_ref, h2t_ref, qp_ref):
    a = jnp.dot(attn_ref[...], wao_ref[...], preferred_element_type=F32)
    merged = conv_ref[...].astype(F32) + _sigmoid(ga_ref[...].astype(F32)) * a
    x1 = x_ref[...] + jnp.dot(merged.astype(BF16), wout_ref[...], preferred_element_type=F32)
    x1_ref[...] = x1
    ms = jnp.mean(x1 * x1, axis=-1, keepdims=True)
    h2 = x1 * lax.rsqrt(ms + EPS) * g2_ref[...]
    h2t_ref[...] = h2.T.astype(BF16)
    qp_ref[...] = jnp.dot(h2.astype(BF16), wq_ref[...],
                          preferred_element_type=F32).astype(qp_ref.dtype)


def _merge(convg, attn, proj, x, wao, wout, g2, wq, tm, gattn_col_block):
    T, D = x.shape
    row = lambda i: (i, 0)
    const = lambda i: (0, 0)
    resident = functools.partial(pl.BlockSpec, pipeline_mode=pl.Buffered(1))
    return pl.pallas_call(
        _merge_kernel,
        grid=(T // tm,),
        in_specs=[
            pl.BlockSpec((tm, D), row),
            pl.BlockSpec((tm, D), row),
            pl.BlockSpec((tm, D), lambda i: (i, gattn_col_block)),
            pl.BlockSpec((tm, D), row),
            resident(wao.shape, const),
            resident(wout.shape, const),
            pl.BlockSpec((1, D), const),
            resident(wq.shape, const),
        ],
        out_specs=[pl.BlockSpec((tm, D), row), pl.BlockSpec((D, tm), lambda i: (0, i)),
                   pl.BlockSpec((tm, wq.shape[1]), row)],
        out_shape=[jax.ShapeDtypeStruct((T, D), F32), jax.ShapeDtypeStruct((D, T), BF16),
                   jax.ShapeDtypeStruct((T, wq.shape[1]), BF16)],
        compiler_params=_params(("parallel",)),
        name="merge_outproj",
    )(convg, attn, proj, x, wao, wout, g2, wq)


def _sort_network(n):
    pairs = []
    p = 1
    while p < n:
        k = p
        while k >= 1:
            for j in range(k % p, n - k, 2 * k):
                for i in range(min(k, n - j - k)):
                    if (i + j) // (2 * p) == (i + j + k) // (2 * p):
                        pairs.append((i + j, i + j + k))
            k //= 2
        p *= 2
    return pairs


def _compare_exchange(v, i, j):
    a, b = v[i], v[j]
    if a is None:
        v[i], v[j] = b, None
    elif b is not None:
        v[i], v[j] = jnp.maximum(a, b), jnp.minimum(a, b)


def _sort_desc(v):
    v = list(v)
    for i, j in _sort_network(len(v)):
        _compare_exchange(v, i, j)
    return v


def _bitonic_merge_desc(v):
    v = list(v)
    k = len(v) // 2
    while k >= 1:
        for i in range(len(v)):
            if i & k == 0:
                _compare_exchange(v, i, i + k)
        k //= 2
    return v


def _top_across_sublanes(v):
    n = len(v)
    for shift in (4, 2, 1):
        other = [None if x is None else pltpu.roll(x, shift, axis=0) for x in v]
        merged = []
        for i in range(n):
            a, b = v[i], other[n - 1 - i]
            merged.append(b if a is None else a if b is None else jnp.maximum(a, b))
        v = _bitonic_merge_desc(merged)
    return v


def _sublane_pack(vals, sub):
    out = vals[0]
    for k in range(1, 8):
        out = jnp.where(sub == k, vals[k], out)
    return out


def _route_chunk(s1, s2):
    n = PEER_TOPK
    v1 = _top_across_sublanes(_sort_desc(s1))
    v2 = _top_across_sublanes(_sort_desc(s2))
    sub = lax.broadcasted_iota(jnp.int32, v1[0].shape, 0)
    v2_lo, v2_hi = _sublane_pack(v2[:8], sub), _sublane_pack(v2[8:], sub)
    v1_hi = _sublane_pack(v1[8:], sub)
    cand = [v1[0] + v2_lo, v1[0] + v2_hi] + [v1[i] + v2_lo for i in range(1, 8)] + [v1_hi + v2[0]]
    cand = _top_across_sublanes(_sort_desc(cand + [None] * (n - len(cand))))
    top, tau = cand[0], cand[n - 1]
    z = None
    for c in cand:
        e = jnp.exp(c - top)
        z = e if z is None else z + e
    cutv = []
    for i in range(n):
        cv = jnp.full_like(top, jnp.inf)
        for j in range(n // (i + 1)):
            cv = jnp.where(v1[i] + v2[j] >= tau, v2[j], cv)
        cutv.append(cv)
    cuts = []
    for g in s1:
        cut = jnp.full_like(g, jnp.inf)
        for i in reversed(range(n)):
            cut = jnp.where(g >= v1[i], cutv[i], cut)
        cuts.append(cut)
    return cuts, v1[0], v2[0], 1.0 / z


def _route_kernel(q_ref, k1_ref, k2_ref, cut_ref, r1_ref, s2_ref, e2_ref):
    nt = (((1,), (1,)), ((), ()))
    groups = PEER_KEYS // 8
    for h in range(PEER_HEADS):
        base = h * 2 * PEER_HALF
        s1 = lax.dot_general(k1_ref[...], q_ref[:, base:base + PEER_HALF], nt,
                             preferred_element_type=F32)
        s2 = lax.dot_general(k2_ref[...], q_ref[:, base + PEER_HALF:base + 2 * PEER_HALF], nt,
                             preferred_element_type=F32)
        for c in range(cut_ref.shape[1]):
            lanes = slice(c * PEER_LANES, (c + 1) * PEER_LANES)
            g1 = [s1[8 * g:8 * g + 8, lanes] for g in range(groups)]
            g2 = [s2[8 * g:8 * g + 8, lanes] for g in range(groups)]
            cuts, m1, m2, rz = _route_chunk(g1, g2)
            for g in range(groups):
                rows = slice(8 * g, 8 * g + 8)
                cut_ref[h, c, rows, :] = cuts[g]
                r1_ref[h, c, rows, :] = jnp.exp(g1[g] - m1)
                s2_ref[h, c, rows, :] = g2[g]
                e2_ref[h, c, rows, :] = jnp.exp(g2[g] - m2) * rz


def _route(qp, k1, k2, tr):
    T = qp.shape[0]
    out = jax.ShapeDtypeStruct((PEER_HEADS, T // PEER_LANES, PEER_KEYS, PEER_LANES), F32)
    ospec = pl.BlockSpec((PEER_HEADS, tr // PEER_LANES, PEER_KEYS, PEER_LANES), lambda i: (0, i, 0, 0))
    return pl.pallas_call(
        _route_kernel,
        grid=(T // tr,),
        in_specs=[
            pl.BlockSpec((tr, qp.shape[1]), lambda i: (i, 0)),
            pl.BlockSpec(k1.shape, lambda i: (0, 0)),
            pl.BlockSpec(k2.shape, lambda i: (0, 0)),
        ],
        out_specs=[ospec] * 4,
        out_shape=[out] * 4,
        compiler_params=_params(("parallel",)),
        name="peer_route",
    )(qp, k1, k2)


def _gelu(a):
    return 0.5 * a * (1.0 + lax.erf(a * (2.0 ** -0.5)))


def _gate_tile(cut_ref, r1_ref, s2_ref, e2_ref, w_ref, tile, n_e1):
    e1_rows = pl.ds(pl.multiple_of(tile * n_e1, n_e1), n_e1)
    for c in range(cut_ref.shape[1]):
        lanes = slice(c * PEER_LANES, (c + 1) * PEER_LANES)
        cuts = [cut_ref[h, c, e1_rows, :] for h in range(PEER_HEADS)]
        r1s = [r1_ref[h, c, e1_rows, :] for h in range(PEER_HEADS)]
        for r in range(n_e1):
            w = None
            for h in range(PEER_HEADS):
                t = jnp.where(s2_ref[h, c] >= cuts[h][r:r + 1], e2_ref[h, c], 0.0)
                t = t * r1s[h][r:r + 1]
                w = t if w is None else w + t
            w_ref[r * PEER_KEYS:(r + 1) * PEER_KEYS, lanes] = w


def _peer_kernel(h2t_ref, u_ref, vt_ref, cut_ref, r1_ref, s2_ref, e2_ref, x1_ref, o_ref,
                 y_scr, a_scr, w_scr, p_scr):
    j = pl.program_id(1)
    nj = pl.num_programs(1)
    te = u_ref.shape[0]
    n_e1 = te // PEER_KEYS
    gate = functools.partial(_gate_tile, cut_ref, r1_ref, s2_ref, e2_ref, n_e1=n_e1)

    @pl.when(j == 0)
    def _():
        y_scr[...] = jnp.zeros_like(y_scr)
        gate(w_scr.at[0], 0)

    a_scr[...] = jnp.dot(u_ref[...], h2t_ref[...], preferred_element_type=F32)
    gate(w_scr.at[(j + 1) % 2], jnp.minimum(j + 1, nj - 1))
    p_scr[...] = (_gelu(a_scr[...]) * w_scr[j % 2]).astype(BF16)
    y_scr[...] += jnp.dot(vt_ref[...], p_scr[...], preferred_element_type=F32)

    @pl.when(j == pl.num_programs(1) - 1)
    def _():
        o_ref[...] = x1_ref[...] + y_scr[...].T


def _peer(h2t, u, vt, cut, r1, s2, e2, x1, tm, te):
    D, T = h2t.shape
    E = u.shape[0]
    assert te % (8 * PEER_KEYS) == 0 and tm % PEER_LANES == 0
    once = functools.partial(pl.BlockSpec, pipeline_mode=pl.Buffered(1))
    rspec = once((PEER_HEADS, tm // PEER_LANES, PEER_KEYS, PEER_LANES), lambda i, j: (0, i, 0, 0))
    return pl.pallas_call(
        _peer_kernel,
        grid=(T // tm, E // te),
        in_specs=[
            once((D, tm), lambda i, j: (0, i)),
            pl.BlockSpec((te, D), lambda i, j: (j, 0)),
            pl.BlockSpec((D, te), lambda i, j: (0, j)),
            rspec, rspec, rspec, rspec,
            once((tm, D), lambda i, j: (i, 0)),
        ],
        out_specs=pl.BlockSpec((tm, D), lambda i, j: (i, 0)),
        out_shape=jax.ShapeDtypeStruct((T, D), F32),
        scratch_shapes=[pltpu.VMEM((D, tm), F32), pltpu.VMEM((te, tm), F32),
                        pltpu.VMEM((2, te, tm), F32), pltpu.VMEM((te, tm), BF16)],
        compiler_params=_params(("parallel", "arbitrary")),
        name="peer_experts",
    )(h2t, u, vt, cut, r1, s2, e2, x1)


def kernel(x, norm1_g, w_in, conv_dw_w, conv_dw_b, conv_ln_g, conv_ln_b, w_conv_out, q_norm_g,
           k_norm_g, rel_bias, w_attn_o, w_out, norm2_g, w_query, sub_keys_1, sub_keys_2,
           expert_u, expert_v):
    B, S, D = x.shape
    assert B == 1, "sequence-causal kernels are written for a single sequence"
    T = S
    C = conv_dw_w.shape[-1]
    A = N_HEADS * HEAD_DIM
    assert C == D and A == D and T % 512 == 0
    xt = x.reshape(T, D)
    for l in range(norm1_g.shape[0]):
        row = lambda v: v[l].reshape(1, -1).astype(F32)
        proj = _in_proj(xt, row(norm1_g), w_in[l].astype(BF16), tm=min(1024, T), tn=1024)
        convg = _conv_branch(proj, conv_dw_w[l], row(conv_dw_b), row(conv_ln_g), row(conv_ln_b),
                             w_conv_out[l].astype(BF16), C, ts=256,
                             gconv_col_block=(2 * C + 3 * A) // C)
        attn = _attention(proj, _attn_bias_table(rel_bias[l]), row(q_norm_g), row(k_norm_g),
                          q_col0=2 * C // HEAD_DIM, k_col0=(2 * C + A) // HEAD_DIM,
                          v_col0=(2 * C + 2 * A) // HEAD_DIM)
        x1, h2t, qp = _merge(convg, attn, proj, xt, w_attn_o[l].astype(BF16), w_out[l].astype(BF16),
                             row(norm2_g), w_query[l].astype(BF16), tm=256,
                             gattn_col_block=(2 * C + 3 * A + D) // D)
        cut, r1, s2, e2 = _route(qp, sub_keys_1[l].astype(BF16), sub_keys_2[l].astype(BF16), tr=256)
        xt = _peer(h2t, expert_u[l].astype(BF16), expert_v[l].astype(BF16).T, cut, r1, s2, e2, x1,
                   tm=512, te=1024)
    return xt.reshape(B, S, D)
```

```python
import functools

import numpy as np
import jax
import jax.numpy as jnp
from jax import lax
from jax.experimental import pallas as pl
from jax.experimental.pallas import tpu as pltpu

F32 = jnp.float32
BF16 = jnp.bfloat16

EPS = 1e-6
CHUNK = 64
N_LEFT_CHUNKS = 8
CONV_WIDTH = 31
N_HEADS = 16
HEAD_DIM = 128
MAX_REL = 128
PEER_HEADS = 8
PEER_KEYS = 128
PEER_HALF = 128
PEER_TOPK = 16
MASK_VALUE = -1e30
LANES = 128
PEER_LANES = LANES

VMEM_LIMIT_BYTES = 56 * 1024 * 1024
PEER_VMEM_LIMIT_BYTES = 58 * 1024 * 1024
CONV_HALO = 32
CONV_ROWS = 64
ATTN_QBLOCK = 4 * CHUNK
ATTN_KBLOCKS = 1 + (N_LEFT_CHUNKS * CHUNK) // ATTN_QBLOCK
ATTN_HEADS_PER_STEP = 4


def _params(sem):
    return pltpu.CompilerParams(dimension_semantics=sem, vmem_limit_bytes=VMEM_LIMIT_BYTES)


def _sigmoid(x):
    return 1.0 / (1.0 + jnp.exp(-x))


def _inproj_kernel(x_ref, g_ref, w_ref, hg_ref, o_ref, h_scr, *, norm_tiles):
    j = pl.program_id(1)

    @pl.when(j == 0)
    def _():
        x = x_ref[...]
        ms = jnp.mean(x * x, axis=-1, keepdims=True)
        h_scr[...] = (x * lax.rsqrt(ms + EPS) * g_ref[...]).astype(BF16)

    acc = jnp.dot(h_scr[...], w_ref[...], preferred_element_type=F32)
    is_head_normed = (j >= norm_tiles[0]) & (j < norm_tiles[1])

    @pl.when(jnp.logical_not(is_head_normed))
    def _():
        o_ref[...] = acc.astype(o_ref.dtype)

    @pl.when(is_head_normed)
    def _():
        for c in range(0, acc.shape[1], HEAD_DIM):
            t = acc[:, c:c + HEAD_DIM]
            ms = jnp.mean(t * t, axis=-1, keepdims=True)
            o_ref[:, c:c + HEAD_DIM] = (t * lax.rsqrt(ms + EPS) * hg_ref[:, c:c + HEAD_DIM]).astype(o_ref.dtype)


def _in_proj(x, g, w, head_gain, norm_cols, tm, tn):
    T, D = x.shape
    N = w.shape[1]
    assert norm_cols[0] % tn == 0 and norm_cols[1] % tn == 0 and tn % HEAD_DIM == 0
    body = functools.partial(_inproj_kernel, norm_tiles=(norm_cols[0] // tn, norm_cols[1] // tn))
    return pl.pallas_call(
        body,
        grid=(T // tm, N // tn),
        in_specs=[
            pl.BlockSpec((tm, D), lambda i, j: (i, 0)),
            pl.BlockSpec((1, D), lambda i, j: (0, 0)),
            pl.BlockSpec((D, tn), lambda i, j: (0, j)),
            pl.BlockSpec((1, tn), lambda i, j: (0, j)),
        ],
        out_specs=pl.BlockSpec((tm, tn), lambda i, j: (i, j)),
        out_shape=jax.ShapeDtypeStruct((T, N), BF16),
        scratch_shapes=[pltpu.VMEM((tm, D), BF16)],
        compiler_params=_params(("parallel", "arbitrary")),
        name="in_proj",
    )(x, g, w, head_gain)


def _conv_kernel(val_ref, gate_ref, hval_ref, hgate_ref, gc_ref, dww_ref, dwb_ref,
                 lng_ref, lnb_ref, wpw_ref, o_ref, u_scr, y_scr):
    i = pl.program_id(0)
    ts, C = val_ref.shape
    halo = hval_ref.shape[0]
    ncb = C // LANES
    hu = hval_ref[...].astype(F32) * _sigmoid(hgate_ref[...].astype(F32))
    hu = jnp.where(i > 0, hu, 0.0)
    u = val_ref[...].astype(F32) * _sigmoid(gate_ref[...].astype(F32))
    for cb in range(ncb):
        lanes = slice(cb * LANES, (cb + 1) * LANES)
        u_scr[cb, 0:halo, :] = hu[:, lanes]
        u_scr[cb, halo:halo + ts, :] = u[:, lanes]

    off = halo - (CONV_WIDTH - 1)
    for cb in range(ncb):
        lanes = slice(cb * LANES, (cb + 1) * LANES)
        taps = [jnp.broadcast_to(dww_ref[j:j + 1, lanes], (8, LANES)) for j in range(CONV_WIDTH)]
        bias = jnp.broadcast_to(dwb_ref[:, lanes], (8, LANES))
        for rb in range(0, ts, CONV_ROWS):
            acc = [bias] * 8
            for s in range(8 + CONV_WIDTH - 1):
                xs = u_scr[cb, pl.ds(off + rb + s, 8, stride=8), :]
                for q in range(8):
                    if 0 <= s - q < CONV_WIDTH:
                        acc[q] = acc[q] + taps[s - q] * xs
            for q in range(8):
                y_scr[cb, pl.ds(rb + q, 8, stride=8), :] = acc[q]

    y = jnp.concatenate([y_scr[cb] for cb in range(ncb)], axis=1)
    mu = jnp.mean(y, axis=-1, keepdims=True)
    yc = y - mu
    var = jnp.mean(yc * yc, axis=-1, keepdims=True)
    z = yc * lax.rsqrt(var + EPS) * lng_ref[...] + lnb_ref[...]
    z = z * _sigmoid(z)
    out = jnp.dot(z.astype(BF16), wpw_ref[...], preferred_element_type=F32)
    o_ref[...] = (out * _sigmoid(gc_ref[...].astype(F32))).astype(o_ref.dtype)


def _conv_branch(proj, dww, dwb, lng, lnb, wpw, C, ts, gconv_col_block):
    T = proj.shape[0]
    assert ts % CONV_ROWS == 0 and C % LANES == 0
    hb = ts // CONV_HALO
    halo_map_v = lambda i: (jnp.maximum(i * hb - 1, 0), 0)
    halo_map_g = lambda i: (jnp.maximum(i * hb - 1, 0), 1)
    const = lambda i: (0, 0)
    return pl.pallas_call(
        _conv_kernel,
        grid=(T // ts,),
        in_specs=[
            pl.BlockSpec((ts, C), lambda i: (i, 0)),
            pl.BlockSpec((ts, C), lambda i: (i, 1)),
            pl.BlockSpec((CONV_HALO, C), halo_map_v),
            pl.BlockSpec((CONV_HALO, C), halo_map_g),
            pl.BlockSpec((ts, C), lambda i: (i, gconv_col_block)),
            pl.BlockSpec((CONV_WIDTH, C), const),
            pl.BlockSpec((1, C), const),
            pl.BlockSpec((1, C), const),
            pl.BlockSpec((1, C), const),
            pl.BlockSpec((C, C), const),
        ],
        out_specs=pl.BlockSpec((ts, C), lambda i: (i, 0)),
        out_shape=jax.ShapeDtypeStruct((T, C), BF16),
        scratch_shapes=[pltpu.VMEM((C // LANES, CONV_HALO + ts, LANES), F32),
                        pltpu.VMEM((C // LANES, ts, LANES), F32)],
        compiler_params=_params(("parallel",)),
        name="conv_branch",
    )(proj, proj, proj, proj, proj, dww, dwb, lng, lnb, wpw)


def _attn_kernel(q_ref, *rest):
    k_refs = rest[:ATTN_KBLOCKS]
    v_refs = rest[ATTN_KBLOCKS:2 * ATTN_KBLOCKS]
    bias_ref, o_ref = rest[2 * ATTN_KBLOCKS:]
    i = pl.program_id(1)
    qb = q_ref.shape[0]
    nkb = ATTN_KBLOCKS
    col = lax.broadcasted_iota(jnp.int32, (1, nkb * qb), 1)
    in_seq = col >= (nkb - 1 - i) * qb
    for hh in range(ATTN_HEADS_PER_STEP):
        lanes = slice(hh * HEAD_DIM, (hh + 1) * HEAD_DIM)
        k = jnp.concatenate([r[:, lanes] for r in k_refs], axis=0)
        v = jnp.concatenate([r[:, lanes] for r in v_refs], axis=0)
        s = lax.dot_general(q_ref[:, lanes], k, (((1,), (1,)), ((), ())), preferred_element_type=F32)
        s = jnp.where(in_seq, s + bias_ref[hh], MASK_VALUE)
        m = jnp.max(s, axis=-1, keepdims=True)
        p = jnp.exp(s - m)
        l = jnp.sum(p, axis=-1, keepdims=True)
        o = jnp.dot(p.astype(BF16), v, preferred_element_type=F32)
        o_ref[:, lanes] = (o / l).astype(o_ref.dtype)


def _attn_bias_table(rel_bias):
    qb, kb = ATTN_QBLOCK, ATTN_KBLOCKS * ATTN_QBLOCK
    n = 1 << int(np.ceil(np.log2(qb + kb)))
    dist = (kb - qb) - (np.arange(n) - (qb - 1))
    f = rel_bias[:, np.clip(dist, -(CHUNK - 1), MAX_REL) + (CHUNK - 1)].astype(F32)
    flat = jnp.tile(f, (1, qb))[:, :qb * (n - 1)]
    bias = flat.reshape(-1, qb, n - 1)[:, :, qb - 1:qb - 1 + kb]
    qpos = np.arange(qb)[:, None]
    kpos = np.arange(kb)[None, :] - (kb - qb)
    qc = qpos // CHUNK
    kc = np.floor_divide(kpos, CHUNK)
    valid = (kc <= qc) & (kc >= qc - N_LEFT_CHUNKS)
    return jnp.where(valid[None], bias, MASK_VALUE)


def _attention(proj, bias, q_col0, k_col0, v_col0):
    T = proj.shape[0]
    qb = ATTN_QBLOCK
    nkb = ATTN_KBLOCKS
    hp = ATTN_HEADS_PER_STEP
    w = hp * HEAD_DIM
    assert N_HEADS % hp == 0 and q_col0 % hp == 0 and k_col0 % hp == 0 and v_col0 % hp == 0

    def kv_spec(col0, back):
        return pl.BlockSpec((qb, w), lambda g, i: (jnp.maximum(i - back, 0), col0 // hp + g))

    k_specs = [kv_spec(k_col0, nkb - 1 - b) for b in range(nkb)]
    v_specs = [kv_spec(v_col0, nkb - 1 - b) for b in range(nkb)]
    return pl.pallas_call(
        _attn_kernel,
        grid=(N_HEADS // hp, T // qb),
        in_specs=[pl.BlockSpec((qb, w), lambda g, i: (i, q_col0 // hp + g))] + k_specs + v_specs + [
            pl.BlockSpec((hp, qb, nkb * qb), lambda g, i: (g, 0, 0)),
        ],
        out_specs=pl.BlockSpec((qb, w), lambda g, i: (i, g)),
        out_shape=jax.ShapeDtypeStruct((T, N_HEADS * HEAD_DIM), BF16),
        compiler_params=_params(("parallel", "arbitrary")),
        name="chunk_attention",
    )(*([proj] * (1 + 2 * nkb)), bias)


def _merge_kernel(conv_ref, attn_ref, ga_ref, x_ref, wao_ref, wout_ref, g2_ref, wq_ref,
                  x1_ref, h2t_ref, qp_ref):
    a = jnp.dot(attn_ref[...], wao_ref[...], preferred_element_type=F32)
    merged = conv_ref[...].astype(F32) + _sigmoid(ga_ref[...].astype(F32)) * a
    x1 = x_ref[...] + jnp.dot(merged.astype(BF16), wout_ref[...], preferred_element_type=F32)
    x1_ref[...] = x1
    ms = jnp.mean(x1 * x1, axis=-1, keepdims=True)
    h2 = x1 * lax.rsqrt(ms + EPS) * g2_ref[...]
    h2t_ref[...] = h2.T.astype(BF16)
    qp_ref[...] = jnp.dot(h2.astype(BF16), wq_ref[...],
                          preferred_element_type=F32).astype(qp_ref.dtype)


def _merge(convg, attn, proj, x, wao, wout, g2, wq, tm, gattn_col_block):
    T, D = x.shape
    row = lambda i: (i, 0)
    const = lambda i: (0, 0)
    resident = functools.partial(pl.BlockSpec, pipeline_mode=pl.Buffered(1))
    return pl.pallas_call(
        _merge_kernel,
        grid=(T // tm,),
        in_specs=[
            pl.BlockSpec((tm, D), row),
            pl.BlockSpec((tm, D), row),
            pl.BlockSpec((tm, D), lambda i: (i, gattn_col_block)),
            pl.BlockSpec((tm, D), row),
            resident(wao.shape, const),
            resident(wout.shape, const),
            pl.BlockSpec((1, D), const),
            resident(wq.shape, const),
        ],
        out_specs=[pl.BlockSpec((tm, D), row), pl.BlockSpec((D, tm), lambda i: (0, i)),
                   pl.BlockSpec((tm, wq.shape[1]), row)],
        out_shape=[jax.ShapeDtypeStruct((T, D), F32), jax.ShapeDtypeStruct((D, T), BF16),
                   jax.ShapeDtypeStruct((T, wq.shape[1]), BF16)],
        compiler_params=_params(("parallel",)),
        name="merge_outproj",
    )(convg, attn, proj, x, wao, wout, g2, wq)


def _sort_network(n):
    pairs = []
    p = 1
    while p < n:
        k = p
        while k >= 1:
            for j in range(k % p, n - k, 2 * k):
                for i in range(min(k, n - j - k)):
                    if (i + j) // (2 * p) == (i + j + k) // (2 * p):
                        pairs.append((i + j, i + j + k))
            k //= 2
        p *= 2
    return pairs


def _compare_exchange(v, i, j):
    a, b = v[i], v[j]
    if a is None:
        v[i], v[j] = b, None
    elif b is not None:
        v[i], v[j] = jnp.maximum(a, b), jnp.minimum(a, b)


def _sort_desc(v):
    v = list(v)
    for i, j in _sort_network(len(v)):
        _compare_exchange(v, i, j)
    return v


def _bitonic_merge_desc(v):
    v = list(v)
    k = len(v) // 2
    while k >= 1:
        for i in range(len(v)):
            if i & k == 0:
                _compare_exchange(v, i, i + k)
        k //= 2
    return v


def _top_across_sublanes(v):
    n = len(v)
    for shift in (4, 2, 1):
        other = [None if x is None else pltpu.roll(x, shift, axis=0) for x in v]
        merged = []
        for i in range(n):
            a, b = v[i], other[n - 1 - i]
            merged.append(b if a is None else a if b is None else jnp.maximum(a, b))
        v = _bitonic_merge_desc(merged)
    return v


def _sublane_pack(vals, sub):
    out = vals[0]
    for k in range(1, 8):
        out = jnp.where(sub == k, vals[k], out)
    return out


def _route_chunk(s1, s2):
    n = PEER_TOPK
    v1 = _top_across_sublanes(_sort_desc(s1))
    v2 = _top_across_sublanes(_sort_desc(s2))
    sub = lax.broadcasted_iota(jnp.int32, v1[0].shape, 0)
    v2_lo, v2_hi = _sublane_pack(v2[:8], sub), _sublane_pack(v2[8:], sub)
    v1_hi = _sublane_pack(v1[8:], sub)
    cand = [v1[0] + v2_lo, v1[0] + v2_hi] + [v1[i] + v2_lo for i in range(1, 8)] + [v1_hi + v2[0]]
    cand = _top_across_sublanes(_sort_desc(cand + [None] * (n - len(cand))))
    top, tau = cand[0], cand[n - 1]
    z = None
    for c in cand:
        e = jnp.exp(c - top)
        z = e if z is None else z + e
    cutv = []
    for i in range(n):
        cv = jnp.full_like(top, jnp.inf)
        for j in range(n // (i + 1)):
            cv = jnp.where(v1[i] + v2[j] >= tau, v2[j], cv)
        cutv.append(cv)
    cuts = []
    for g in s1:
        cut = jnp.full_like(g, jnp.inf)
        for i in reversed(range(n)):
            cut = jnp.where(g >= v1[i], cutv[i], cut)
        cuts.append(cut)
    return cuts, v1[0], v2[0], 1.0 / z


def _route_kernel(q_ref, k1_ref, k2_ref, cut_ref, r1_ref, s2_ref, e2_ref):
    nt = (((1,), (1,)), ((), ()))
    groups = PEER_KEYS // 8
    for h in range(PEER_HEADS):
        base = h * 2 * PEER_HALF
        s1 = lax.dot_general(k1_ref[...], q_ref[:, base:base + PEER_HALF], nt,
                             preferred_element_type=F32)
        s2 = lax.dot_general(k2_ref[...], q_ref[:, base + PEER_HALF:base + 2 * PEER_HALF], nt,
                             preferred_element_type=F32)
        for c in range(cut_ref.shape[1]):
            lanes = slice(c * PEER_LANES, (c + 1) * PEER_LANES)
            g1 = [s1[8 * g:8 * g + 8, lanes] for g in range(groups)]
            g2 = [s2[8 * g:8 * g + 8, lanes] for g in range(groups)]
            cuts, m1, m2, rz = _route_chunk(g1, g2)
            for g in range(groups):
                rows = slice(8 * g, 8 * g + 8)
                cut_ref[h, c, rows, :] = cuts[g]
                r1_ref[h, c, rows, :] = jnp.exp(g1[g] - m1)
                s2_ref[h, c, rows, :] = g2[g]
                e2_ref[h, c, rows, :] = jnp.exp(g2[g] - m2) * rz


def _route(qp, k1, k2, tr):
    T = qp.shape[0]
    out = jax.ShapeDtypeStruct((PEER_HEADS, T // PEER_LANES, PEER_KEYS, PEER_LANES), F32)
    ospec = pl.BlockSpec((PEER_HEADS, tr // PEER_LANES, PEER_KEYS, PEER_LANES), lambda i: (0, i, 0, 0))
    return pl.pallas_call(
        _route_kernel,
        grid=(T // tr,),
        in_specs=[
            pl.BlockSpec((tr, qp.shape[1]), lambda i: (i, 0)),
            pl.BlockSpec(k1.shape, lambda i: (0, 0)),
            pl.BlockSpec(k2.shape, lambda i: (0, 0)),
        ],
        out_specs=[ospec] * 4,
        out_shape=[out] * 4,
        compiler_params=_params(("parallel",)),
        name="peer_route",
    )(qp, k1, k2)


def _gelu(a):
    return 0.5 * a * (1.0 + lax.erf(a * (2.0 ** -0.5)))


GATE_ROWS = 64
GATE_E1 = 2


def _gate_tile(cut_ref, r1_ref, s2_ref, e2_ref, w_ref, tile, n_e1):
    e1_rows = pl.ds(pl.multiple_of(tile * n_e1, n_e1), n_e1)
    for c in range(cut_ref.shape[1]):
        lanes = slice(c * PEER_LANES, (c + 1) * PEER_LANES)
        cuts = [cut_ref[h, c, e1_rows, :] for h in range(PEER_HEADS)]
        r1s = [r1_ref[h, c, e1_rows, :] for h in range(PEER_HEADS)]
        for r0 in range(0, n_e1, GATE_E1):
            for k0 in range(0, PEER_KEYS, GATE_ROWS):
                keys = slice(k0, k0 + GATE_ROWS)
                w = [None] * GATE_E1
                for h in range(PEER_HEADS):
                    s2v, e2v = s2_ref[h, c, keys, :], e2_ref[h, c, keys, :]
                    for d in range(GATE_E1):
                        r = r0 + d
                        t = jnp.where(s2v >= cuts[h][r:r + 1], e2v, 0.0) * r1s[h][r:r + 1]
                        w[d] = t if w[d] is None else w[d] + t
                for d in range(GATE_E1):
                    row0 = (r0 + d) * PEER_KEYS + k0
                    w_ref[row0:row0 + GATE_ROWS, lanes] = w[d]


def _peer_kernel(h2t_ref, u_ref, vt_ref, cut_ref, r1_ref, s2_ref, e2_ref, x1_ref, o_ref,
                 y_scr, a_scr, w_scr, p_scr):
    j = pl.program_id(1)
    n_tiles = pl.num_programs(1) - 1
    n_e1 = u_ref.shape[0] // PEER_KEYS
    cons, prod = (j + 1) % 2, j % 2

    @pl.when(j == 0)
    def _():
        y_scr[...] = jnp.zeros_like(y_scr)
        a_scr[1] = jnp.zeros(a_scr.shape[1:], a_scr.dtype)
        w_scr[1] = jnp.zeros(w_scr.shape[1:], w_scr.dtype)

    p_scr[...] = (_gelu(a_scr[cons]) * w_scr[cons]).astype(BF16)
    y_scr[...] += jnp.dot(vt_ref[...], p_scr[...], preferred_element_type=F32)
    _gate_tile(cut_ref, r1_ref, s2_ref, e2_ref, w_scr.at[prod], jnp.minimum(j, n_tiles - 1), n_e1)
    a_scr[prod] = jnp.dot(u_ref[...], h2t_ref[...], preferred_element_type=F32)

    @pl.when(j == n_tiles)
    def _():
        o_ref[...] = x1_ref[...] + y_scr[...].T


def _peer(h2t, u, vt, cut, r1, s2, e2, x1, tm, te):
    D, T = h2t.shape
    E = u.shape[0]
    n_tiles = E // te
    assert te % (8 * PEER_KEYS) == 0 and tm % PEER_LANES == 0
    once = functools.partial(pl.BlockSpec, pipeline_mode=pl.Buffered(1))
    rspec = once((PEER_HEADS, tm // PEER_LANES, PEER_KEYS, PEER_LANES), lambda i, j: (0, i, 0, 0))
    return pl.pallas_call(
        _peer_kernel,
        grid=(T // tm, n_tiles + 1),
        in_specs=[
            once((D, tm), lambda i, j: (0, i)),
            pl.BlockSpec((te, D), lambda i, j: (jnp.minimum(j, n_tiles - 1), 0)),
            pl.BlockSpec((D, te), lambda i, j: (0, jnp.maximum(j - 1, 0))),
            rspec, rspec, rspec, rspec,
            once((tm, D), lambda i, j: (i, 0)),
        ],
        out_specs=pl.BlockSpec((tm, D), lambda i, j: (i, 0)),
        out_shape=jax.ShapeDtypeStruct((T, D), F32),
        scratch_shapes=[pltpu.VMEM((D, tm), F32), pltpu.VMEM((2, te, tm), F32),
                        pltpu.VMEM((2, te, tm), F32), pltpu.VMEM((te, tm), BF16)],
        compiler_params=pltpu.CompilerParams(dimension_semantics=("parallel", "arbitrary"),
                                             vmem_limit_bytes=PEER_VMEM_LIMIT_BYTES),
        name="peer_experts",
    )(h2t, u, vt, cut, r1, s2, e2, x1)


def kernel(x, norm1_g, w_in, conv_dw_w, conv_dw_b, conv_ln_g, conv_ln_b, w_conv_out, q_norm_g,
           k_norm_g, rel_bias, w_attn_o, w_out, norm2_g, w_query, sub_keys_1, sub_keys_2,
           expert_u, expert_v):
    B, S, D = x.shape
    assert B == 1, "sequence-causal kernels are written for a single sequence"
    T = S
    C = conv_dw_w.shape[-1]
    A = N_HEADS * HEAD_DIM
    assert C == D and A == D and T % 512 == 0
    xt = x.reshape(T, D)
    for l in range(norm1_g.shape[0]):
        row = lambda v: v[l].reshape(1, -1).astype(F32)
        head_gain = jnp.concatenate([
            jnp.ones((1, 2 * C), F32),
            jnp.tile(row(q_norm_g) * (HEAD_DIM ** -0.5), (1, N_HEADS)),
            jnp.tile(row(k_norm_g), (1, N_HEADS)),
            jnp.ones((1, A + 2 * D), F32)], axis=1)
        proj = _in_proj(xt, row(norm1_g), w_in[l].astype(BF16), head_gain, (2 * C, 2 * C + 2 * A),
                        tm=min(1024, T), tn=1024)
        convg = _conv_branch(proj, conv_dw_w[l], row(conv_dw_b), row(conv_ln_g), row(conv_ln_b),
                             w_conv_out[l].astype(BF16), C, ts=256,
                             gconv_col_block=(2 * C + 3 * A) // C)
        attn = _attention(proj, _attn_bias_table(rel_bias[l]),
                          q_col0=2 * C // HEAD_DIM, k_col0=(2 * C + A) // HEAD_DIM,
                          v_col0=(2 * C + 2 * A) // HEAD_DIM)
        x1, h2t, qp = _merge(convg, attn, proj, xt, w_attn_o[l].astype(BF16), w_out[l].astype(BF16),
                             row(norm2_g), w_query[l].astype(BF16), tm=256,
                             gattn_col_block=(2 * C + 3 * A + D) // D)
        cut, r1, s2, e2 = _route(qp, sub_keys_1[l].astype(BF16), sub_keys_2[l].astype(BF16), tr=256)
        xt = _peer(h2t, expert_u[l].astype(BF16), expert_v[l].astype(BF16).T, cut, r1, s2, e2, x1,
                   tm=512, te=1024)
    return xt.reshape(B, S, D)
```

```python
import functools

import numpy as np
import jax
import jax.numpy as jnp
from jax import lax
from jax.experimental import pallas as pl
from jax.experimental.pallas import tpu as pltpu

F32 = jnp.float32
BF16 = jnp.bfloat16

EPS = 1e-6
CHUNK = 64
N_LEFT_CHUNKS = 8
CONV_WIDTH = 31
N_HEADS = 16
HEAD_DIM = 128
MAX_REL = 128
PEER_HEADS = 8
PEER_KEYS = 128
PEER_HALF = 128
PEER_TOPK = 16
MASK_VALUE = -1e30
LANES = 128
PEER_LANES = LANES

VMEM_LIMIT_BYTES = 56 * 1024 * 1024
CONV_HALO = 32
CONV_ROWS = 64
ATTN_QBLOCK = 4 * CHUNK
ATTN_KBLOCKS = 1 + (N_LEFT_CHUNKS * CHUNK) // ATTN_QBLOCK
ATTN_HEADS_PER_STEP = 8


def _params(sem):
    return pltpu.CompilerParams(dimension_semantics=sem, vmem_limit_bytes=VMEM_LIMIT_BYTES)


def _sigmoid(x):
    return 1.0 / (1.0 + jnp.exp(-x))


def _inproj_kernel(x_ref, g_ref, w_ref, o_ref, h_scr):
    @pl.when(pl.program_id(1) == 0)
    def _():
        x = x_ref[...]
        ms = jnp.mean(x * x, axis=-1, keepdims=True)
        h_scr[...] = (x * lax.rsqrt(ms + EPS) * g_ref[...]).astype(BF16)

    o_ref[...] = jnp.dot(h_scr[...], w_ref[...], preferred_element_type=F32).astype(o_ref.dtype)


def _in_proj(x, g, w, tm, tn):
    T, D = x.shape
    N = w.shape[1]
    return pl.pallas_call(
        _inproj_kernel,
        grid=(T // tm, N // tn),
        in_specs=[
            pl.BlockSpec((tm, D), lambda i, j: (i, 0)),
            pl.BlockSpec((1, D), lambda i, j: (0, 0)),
            pl.BlockSpec((D, tn), lambda i, j: (0, j)),
        ],
        out_specs=pl.BlockSpec((tm, tn), lambda i, j: (i, j)),
        out_shape=jax.ShapeDtypeStruct((T, N), BF16),
        scratch_shapes=[pltpu.VMEM((tm, D), BF16)],
        compiler_params=_params(("parallel", "arbitrary")),
        name="in_proj",
    )(x, g, w)


def _conv_kernel(val_ref, gate_ref, hval_ref, hgate_ref, gc_ref, dww_ref, dwb_ref,
                 lng_ref, lnb_ref, wpw_ref, o_ref, u_scr, y_scr):
    i = pl.program_id(0)
    ts, C = val_ref.shape
    halo = hval_ref.shape[0]
    ncb = C // LANES
    hu = hval_ref[...].astype(F32) * _sigmoid(hgate_ref[...].astype(F32))
    hu = jnp.where(i > 0, hu, 0.0)
    u = val_ref[...].astype(F32) * _sigmoid(gate_ref[...].astype(F32))
    for cb in range(ncb):
        lanes = slice(cb * LANES, (cb + 1) * LANES)
        u_scr[cb, 0:halo, :] = hu[:, lanes]
        u_scr[cb, halo:halo + ts, :] = u[:, lanes]

    off = halo - (CONV_WIDTH - 1)
    for cb in range(ncb):
        lanes = slice(cb * LANES, (cb + 1) * LANES)
        taps = [jnp.broadcast_to(dww_ref[j:j + 1, lanes], (8, LANES)) for j in range(CONV_WIDTH)]
        bias = jnp.broadcast_to(dwb_ref[:, lanes], (8, LANES))
        for rb in range(0, ts, CONV_ROWS):
            acc = [bias] * 8
            for s in range(8 + CONV_WIDTH - 1):
                xs = u_scr[cb, pl.ds(off + rb + s, 8, stride=8), :]
                for q in range(8):
                    if 0 <= s - q < CONV_WIDTH:
                        acc[q] = acc[q] + taps[s - q] * xs
            for q in range(8):
                y_scr[cb, pl.ds(rb + q, 8, stride=8), :] = acc[q]

    y = jnp.concatenate([y_scr[cb] for cb in range(ncb)], axis=1)
    mu = jnp.mean(y, axis=-1, keepdims=True)
    yc = y - mu
    var = jnp.mean(yc * yc, axis=-1, keepdims=True)
    z = yc * lax.rsqrt(var + EPS) * lng_ref[...] + lnb_ref[...]
    z = z * _sigmoid(z)
    out = jnp.dot(z.astype(BF16), wpw_ref[...], preferred_element_type=F32)
    o_ref[...] = (out * _sigmoid(gc_ref[...].astype(F32))).astype(o_ref.dtype)


def _conv_branch(proj, dww, dwb, lng, lnb, wpw, C, ts, gconv_col_block):
    T = proj.shape[0]
    assert ts % CONV_ROWS == 0 and C % LANES == 0
    hb = ts // CONV_HALO
    halo_map_v = lambda i: (jnp.maximum(i * hb - 1, 0), 0)
    halo_map_g = lambda i: (jnp.maximum(i * hb - 1, 0), 1)
    const = lambda i: (0, 0)
    return pl.pallas_call(
        _conv_kernel,
        grid=(T // ts,),
        in_specs=[
            pl.BlockSpec((ts, C), lambda i: (i, 0)),
            pl.BlockSpec((ts, C), lambda i: (i, 1)),
            pl.BlockSpec((CONV_HALO, C), halo_map_v),
            pl.BlockSpec((CONV_HALO, C), halo_map_g),
            pl.BlockSpec((ts, C), lambda i: (i, gconv_col_block)),
            pl.BlockSpec((CONV_WIDTH, C), const),
            pl.BlockSpec((1, C), const),
            pl.BlockSpec((1, C), const),
            pl.BlockSpec((1, C), const),
            pl.BlockSpec((C, C), const),
        ],
        out_specs=pl.BlockSpec((ts, C), lambda i: (i, 0)),
        out_shape=jax.ShapeDtypeStruct((T, C), BF16),
        scratch_shapes=[pltpu.VMEM((C // LANES, CONV_HALO + ts, LANES), F32),
                        pltpu.VMEM((C // LANES, ts, LANES), F32)],
        compiler_params=_params(("parallel",)),
        name="conv_branch",
    )(proj, proj, proj, proj, proj, dww, dwb, lng, lnb, wpw)


def _attn_kernel(q_ref, *rest):
    k_refs = rest[:ATTN_KBLOCKS]
    v_refs = rest[ATTN_KBLOCKS:2 * ATTN_KBLOCKS]
    bias_ref, qg_ref, kg_ref, o_ref = rest[2 * ATTN_KBLOCKS:]
    i = pl.program_id(1)
    qb = q_ref.shape[0]
    nkb = ATTN_KBLOCKS

    def rms(t, g):
        t = t.astype(F32)
        return t * lax.rsqrt(jnp.mean(t * t, axis=-1, keepdims=True) + EPS) * g

    col = lax.broadcasted_iota(jnp.int32, (1, nkb * qb), 1)
    in_seq = col >= (nkb - 1 - i) * qb
    for hh in range(ATTN_HEADS_PER_STEP):
        lanes = slice(hh * HEAD_DIM, (hh + 1) * HEAD_DIM)
        q = (rms(q_ref[:, lanes], qg_ref[...]) * (HEAD_DIM ** -0.5)).astype(BF16)
        k = jnp.concatenate([rms(r[:, lanes], kg_ref[...]).astype(BF16) for r in k_refs], axis=0)
        v = jnp.concatenate([r[:, lanes] for r in v_refs], axis=0)
        s = lax.dot_general(q, k, (((1,), (1,)), ((), ())), preferred_element_type=F32)
        s = jnp.where(in_seq, s + bias_ref[hh], MASK_VALUE)
        m = jnp.max(s, axis=-1, keepdims=True)
        p = jnp.exp(s - m)
        l = jnp.sum(p, axis=-1, keepdims=True)
        o = jnp.dot(p.astype(BF16), v, preferred_element_type=F32)
        o_ref[:, lanes] = (o / l).astype(o_ref.dtype)


def _attn_bias_table(rel_bias):
    qb, kb = ATTN_QBLOCK, ATTN_KBLOCKS * ATTN_QBLOCK
    n = 1 << int(np.ceil(np.log2(qb + kb)))
    dist = (kb - qb) - (np.arange(n) - (qb - 1))
    f = rel_bias[:, np.clip(dist, -(CHUNK - 1), MAX_REL) + (CHUNK - 1)].astype(F32)
    flat = jnp.tile(f, (1, qb))[:, :qb * (n - 1)]
    bias = flat.reshape(-1, qb, n - 1)[:, :, qb - 1:qb - 1 + kb]
    qpos = np.arange(qb)[:, None]
    kpos = np.arange(kb)[None, :] - (kb - qb)
    qc = qpos // CHUNK
    kc = np.floor_divide(kpos, CHUNK)
    valid = (kc <= qc) & (kc >= qc - N_LEFT_CHUNKS)
    return jnp.where(valid[None], bias, MASK_VALUE)


def _attention(proj, bias, qg, kg, q_col0, k_col0, v_col0):
    T = proj.shape[0]
    qb = ATTN_QBLOCK
    nkb = ATTN_KBLOCKS
    hp = ATTN_HEADS_PER_STEP
    w = hp * HEAD_DIM
    assert N_HEADS % hp == 0 and q_col0 % hp == 0 and k_col0 % hp == 0 and v_col0 % hp == 0

    def kv_spec(col0, back):
        return pl.BlockSpec((qb, w), lambda g, i: (jnp.maximum(i - back, 0), col0 // hp + g))

    k_specs = [kv_spec(k_col0, nkb - 1 - b) for b in range(nkb)]
    v_specs = [kv_spec(v_col0, nkb - 1 - b) for b in range(nkb)]
    return pl.pallas_call(
        _attn_kernel,
        grid=(N_HEADS // hp, T // qb),
        in_specs=[pl.BlockSpec((qb, w), lambda g, i: (i, q_col0 // hp + g))] + k_specs + v_specs + [
            pl.BlockSpec((hp, qb, nkb * qb), lambda g, i: (g, 0, 0)),
            pl.BlockSpec((1, HEAD_DIM), lambda g, i: (0, 0)),
            pl.BlockSpec((1, HEAD_DIM), lambda g, i: (0, 0)),
        ],
        out_specs=pl.BlockSpec((qb, w), lambda g, i: (i, g)),
        out_shape=jax.ShapeDtypeStruct((T, N_HEADS * HEAD_DIM), BF16),
        compiler_params=_params(("parallel", "arbitrary")),
        name="chunk_attention",
    )(*([proj] * (1 + 2 * nkb)), bias, qg, kg)


def _merge_kernel(conv_ref, attn_ref, ga_ref, x_ref, wao_ref, wout_ref, g2_ref, wq_ref,
                  x1_ref, h2t_ref, qp_ref):
    a = jnp.dot(attn_ref[...], wao_ref[...], preferred_element_type=F32)
    merged = conv_ref[...].astype(F32) + _sigmoid(ga_ref[...].astype(F32)) * a
    x1 = x_ref[...] + jnp.dot(merged.astype(BF16), wout_ref[...], preferred_element_type=F32)
    x1_ref[...] = x1
    ms = jnp.mean(x1 * x1, axis=-1, keepdims=True)
    h2 = x1 * lax.rsqrt(ms + EPS) * g2_ref[...]
    h2t_ref[...] = h2.T.astype(BF16)
    qp_ref[...] = jnp.dot(h2.astype(BF16), wq_ref[...],
                          preferred_element_type=F32).astype(qp_ref.dtype)


def _merge(convg, attn, proj, x, wao, wout, g2, wq, tm, gattn_col_block):
    T, D = x.shape
    row = lambda i: (i, 0)
    const = lambda i: (0, 0)
    resident = functools.partial(pl.BlockSpec, pipeline_mode=pl.Buffered(1))
    return pl.pallas_call(
        _merge_kernel,
        grid=(T // tm,),
        in_specs=[
            pl.BlockSpec((tm, D), row),
            pl.BlockSpec((tm, D), row),
            pl.BlockSpec((tm, D), lambda i: (i, gattn_col_block)),
            pl.BlockSpec((tm, D), row),
            resident(wao.shape, const),
            resident(wout.shape, const),
            pl.BlockSpec((1, D), const),
            resident(wq.shape, const),
        ],
        out_specs=[pl.BlockSpec((tm, D), row), pl.BlockSpec((D, tm), lambda i: (0, i)),
                   pl.BlockSpec((tm, wq.shape[1]), row)],
        out_shape=[jax.ShapeDtypeStruct((T, D), F32), jax.ShapeDtypeStruct((D, T), BF16),
                   jax.ShapeDtypeStruct((T, wq.shape[1]), BF16)],
        compiler_params=_params(("parallel",)),
        name="merge_outproj",
    )(convg, attn, proj, x, wao, wout, g2, wq)


def _sort_network(n):
    pairs = []
    p = 1
    while p < n:
        k = p
        while k >= 1:
            for j in range(k % p, n - k, 2 * k):
                for i in range(min(k, n - j - k)):
                    if (i + j) // (2 * p) == (i + j + k) // (2 * p):
                        pairs.append((i + j, i + j + k))
            k //= 2
        p *= 2
    return pairs


def _compare_exchange(v, i, j):
    a, b = v[i], v[j]
    if a is None:
        v[i], v[j] = b, None
    elif b is not None:
        v[i], v[j] = jnp.maximum(a, b), jnp.minimum(a, b)


def _sort_desc(v):
    v = list(v)
    for i, j in _sort_network(len(v)):
        _compare_exchange(v, i, j)
    return v


def _bitonic_merge_desc(v):
    v = list(v)
    k = len(v) // 2
    while k >= 1:
        for i in range(len(v)):
            if i & k == 0:
                _compare_exchange(v, i, i + k)
        k //= 2
    return v


def _top_across_sublanes(v):
    n = len(v)
    for shift in (4, 2, 1):
        other = [None if x is None else pltpu.roll(x, shift, axis=0) for x in v]
        merged = []
        for i in range(n):
            a, b = v[i], other[n - 1 - i]
            merged.append(b if a is None else a if b is None else jnp.maximum(a, b))
        v = _bitonic_merge_desc(merged)
    return v


def _sublane_pack(vals, sub):
    out = vals[0]
    for k in range(1, 8):
        out = jnp.where(sub == k, vals[k], out)
    return out


def _route_chunk(s1, s2):
    n = PEER_TOPK
    v1 = _top_across_sublanes(_sort_desc(s1))
    v2 = _top_across_sublanes(_sort_desc(s2))
    sub = lax.broadcasted_iota(jnp.int32, v1[0].shape, 0)
    v2_lo, v2_hi = _sublane_pack(v2[:8], sub), _sublane_pack(v2[8:], sub)
    v1_hi = _sublane_pack(v1[8:], sub)
    cand = [v1[0] + v2_lo, v1[0] + v2_hi] + [v1[i] + v2_lo for i in range(1, 8)] + [v1_hi + v2[0]]
    cand = _top_across_sublanes(_sort_desc(cand + [None] * (n - len(cand))))
    top, tau = cand[0], cand[n - 1]
    z = None
    for c in cand:
        e = jnp.exp(c - top)
        z = e if z is None else z + e
    cutv = []
    for i in range(n):
        cv = jnp.full_like(top, jnp.inf)
        for j in range(n // (i + 1)):
            cv = jnp.where(v1[i] + v2[j] >= tau, v2[j], cv)
        cutv.append(cv)
    cuts = []
    for g in s1:
        cut = jnp.full_like(g, jnp.inf)
        for i in reversed(range(n)):
            cut = jnp.where(g >= v1[i], cutv[i], cut)
        cuts.append(cut)
    return cuts, v1[0], v2[0], 1.0 / z


def _route_kernel(q_ref, k1_ref, k2_ref, cut_ref, r1_ref, s2_ref, e2_ref):
    nt = (((1,), (1,)), ((), ()))
    groups = PEER_KEYS // 8
    for h in range(PEER_HEADS):
        base = h * 2 * PEER_HALF
        s1 = lax.dot_general(k1_ref[...], q_ref[:, base:base + PEER_HALF], nt,
                             preferred_element_type=F32)
        s2 = lax.dot_general(k2_ref[...], q_ref[:, base + PEER_HALF:base + 2 * PEER_HALF], nt,
                             preferred_element_type=F32)
        for c in range(cut_ref.shape[1]):
            lanes = slice(c * PEER_LANES, (c + 1) * PEER_LANES)
            g1 = [s1[8 * g:8 * g + 8, lanes] for g in range(groups)]
            g2 = [s2[8 * g:8 * g + 8, lanes] for g in range(groups)]
            cuts, m1, m2, rz = _route_chunk(g1, g2)
            for g in range(groups):
                rows = slice(8 * g, 8 * g + 8)
                cut_ref[h, c, rows, :] = cuts[g]
                r1_ref[h, c, rows, :] = jnp.exp(g1[g] - m1)
                s2_ref[h, c, rows, :] = g2[g]
                e2_ref[h, c, rows, :] = jnp.exp(g2[g] - m2) * rz


def _route(qp, k1, k2, tr):
    T = qp.shape[0]
    out = jax.ShapeDtypeStruct((PEER_HEADS, T // PEER_LANES, PEER_KEYS, PEER_LANES), F32)
    ospec = pl.BlockSpec((PEER_HEADS, tr // PEER_LANES, PEER_KEYS, PEER_LANES), lambda i: (0, i, 0, 0))
    return pl.pallas_call(
        _route_kernel,
        grid=(T // tr,),
        in_specs=[
            pl.BlockSpec((tr, qp.shape[1]), lambda i: (i, 0)),
            pl.BlockSpec(k1.shape, lambda i: (0, 0)),
            pl.BlockSpec(k2.shape, lambda i: (0, 0)),
        ],
        out_specs=[ospec] * 4,
        out_shape=[out] * 4,
        compiler_params=_params(("parallel",)),
        name="peer_route",
    )(qp, k1, k2)


def _gelu(a):
    return 0.5 * a * (1.0 + lax.erf(a * (2.0 ** -0.5)))


def _gate_tile(cut_ref, r1_ref, s2_ref, e2_ref, w_ref, tile, n_e1):
    e1_rows = pl.ds(pl.multiple_of(tile * n_e1, n_e1), n_e1)
    for c in range(cut_ref.shape[1]):
        lanes = slice(c * PEER_LANES, (c + 1) * PEER_LANES)
        cuts = [cut_ref[h, c, e1_rows, :] for h in range(PEER_HEADS)]
        r1s = [r1_ref[h, c, e1_rows, :] for h in range(PEER_HEADS)]
        for r in range(n_e1):
            w = None
            for h in range(PEER_HEADS):
                t = jnp.where(s2_ref[h, c] >= cuts[h][r:r + 1], e2_ref[h, c], 0.0)
                t = t * r1s[h][r:r + 1]
                w = t if w is None else w + t
            w_ref[r * PEER_KEYS:(r + 1) * PEER_KEYS, lanes] = w


def _peer_kernel(h2t_ref, u_ref, vt_ref, cut_ref, r1_ref, s2_ref, e2_ref, x1_ref, o_ref,
                 y_scr, a_scr, w_scr, p_scr):
    j = pl.program_id(1)
    nj = pl.num_programs(1)
    te = u_ref.shape[0]
    n_e1 = te // PEER_KEYS
    gate = functools.partial(_gate_tile, cut_ref, r1_ref, s2_ref, e2_ref, n_e1=n_e1)

    @pl.when(j == 0)
    def _():
        y_scr[...] = jnp.zeros_like(y_scr)
        gate(w_scr.at[0], 0)

    a_scr[...] = jnp.dot(u_ref[...], h2t_ref[...], preferred_element_type=F32)
    gate(w_scr.at[(j + 1) % 2], jnp.minimum(j + 1, nj - 1))
    p_scr[...] = (_gelu(a_scr[...]) * w_scr[j % 2]).astype(BF16)
    y_scr[...] += jnp.dot(vt_ref[...], p_scr[...], preferred_element_type=F32)

    @pl.when(j == pl.num_programs(1) - 1)
    def _():
        o_ref[...] = x1_ref[...] + y_scr[...].T


def _peer(h2t, u, vt, cut, r1, s2, e2, x1, tm, te):
    D, T = h2t.shape
    E = u.shape[0]
    assert te % (8 * PEER_KEYS) == 0 and tm % PEER_LANES == 0
    once = functools.partial(pl.BlockSpec, pipeline_mode=pl.Buffered(1))
    rspec = once((PEER_HEADS, tm // PEER_LANES, PEER_KEYS, PEER_LANES), lambda i, j: (0, i, 0, 0))
    return pl.pallas_call(
        _peer_kernel,
        grid=(T // tm, E // te),
        in_specs=[
            once((D, tm), lambda i, j: (0, i)),
            pl.BlockSpec((te, D), lambda i, j: (j, 0)),
            pl.BlockSpec((D, te), lambda i, j: (0, j)),
            rspec, rspec, rspec, rspec,
            once((tm, D), lambda i, j: (i, 0)),
        ],
        out_specs=pl.BlockSpec((tm, D), lambda i, j: (i, 0)),
        out_shape=jax.ShapeDtypeStruct((T, D), F32),
        scratch_shapes=[pltpu.VMEM((D, tm), F32), pltpu.VMEM((te, tm), F32),
                        pltpu.VMEM((2, te, tm), F32), pltpu.VMEM((te, tm), BF16)],
        compiler_params=_params(("parallel", "arbitrary")),
        name="peer_experts",
    )(h2t, u, vt, cut, r1, s2, e2, x1)


def _transpose_cast_kernel(v_ref, o_ref):
    o_ref[...] = v_ref[...].T.astype(o_ref.dtype)


def _transpose_cast(v, rows):
    E, D = v.shape
    return pl.pallas_call(
        _transpose_cast_kernel,
        grid=(E // rows,),
        in_specs=[pl.BlockSpec((rows, D), lambda i: (i, 0))],
        out_specs=pl.BlockSpec((D, rows), lambda i: (0, i)),
        out_shape=jax.ShapeDtypeStruct((D, E), BF16),
        compiler_params=_params(("parallel",)),
        name="expert_v_transpose",
    )(v)


def kernel(x, norm1_g, w_in, conv_dw_w, conv_dw_b, conv_ln_g, conv_ln_b, w_conv_out, q_norm_g,
           k_norm_g, rel_bias, w_attn_o, w_out, norm2_g, w_query, sub_keys_1, sub_keys_2,
           expert_u, expert_v):
    B, S, D = x.shape
    assert B == 1, "sequence-causal kernels are written for a single sequence"
    T = S
    C = conv_dw_w.shape[-1]
    A = N_HEADS * HEAD_DIM
    assert C == D and A == D and T % 512 == 0
    xt = x.reshape(T, D)
    for l in range(norm1_g.shape[0]):
        row = lambda v: v[l].reshape(1, -1).astype(F32)
        proj = _in_proj(xt, row(norm1_g), w_in[l].astype(BF16), tm=min(1024, T), tn=1024)
        convg = _conv_branch(proj, conv_dw_w[l], row(conv_dw_b), row(conv_ln_g), row(conv_ln_b),
                             w_conv_out[l].astype(BF16), C, ts=256,
                             gconv_col_block=(2 * C + 3 * A) // C)
        attn = _attention(proj, _attn_bias_table(rel_bias[l]), row(q_norm_g), row(k_norm_g),
                          q_col0=2 * C // HEAD_DIM, k_col0=(2 * C + A) // HEAD_DIM,
                          v_col0=(2 * C + 2 * A) // HEAD_DIM)
        x1, h2t, qp = _merge(convg, attn, proj, xt, w_attn_o[l].astype(BF16), w_out[l].astype(BF16),
                             row(norm2_g), w_query[l].astype(BF16), tm=256,
                             gattn_col_block=(2 * C + 3 * A + D) // D)
        cut, r1, s2, e2 = _route(qp, sub_keys_1[l].astype(BF16), sub_keys_2[l].astype(BF16), tr=256)
        xt = _peer(h2t, expert_u[l].astype(BF16), _transpose_cast(expert_v[l], rows=1024),
                   cut, r1, s2, e2, x1, tm=512, te=1024)
    return xt.reshape(B, S, D)
```

```python
import functools

import numpy as np
import jax
import jax.numpy as jnp
from jax import lax
from jax.experimental import pallas as pl
from jax.experimental.pallas import tpu as pltpu

F32 = jnp.float32
BF16 = jnp.bfloat16

EPS = 1e-6
CHUNK = 64
N_LEFT_CHUNKS = 8
CONV_WIDTH = 31
N_HEADS = 16
HEAD_DIM = 128
MAX_REL = 128
PEER_HEADS = 8
PEER_KEYS = 128
PEER_HALF = 128
PEER_TOPK = 16
MASK_VALUE = -1e30
LOG2E = 1.4426950408889634
LANES = 128
PEER_LANES = LANES

VMEM_LIMIT_BYTES = 56 * 1024 * 1024
CONV_HALO = 32
CONV_ROWS = 64
ATTN_QBLOCK = 4 * CHUNK
ATTN_KBLOCKS = 1 + (N_LEFT_CHUNKS * CHUNK) // ATTN_QBLOCK
ATTN_HEADS_PER_STEP = 16


def _params(sem):
    return pltpu.CompilerParams(dimension_semantics=sem, vmem_limit_bytes=VMEM_LIMIT_BYTES)


def _sigmoid(x):
    return 1.0 / (1.0 + jnp.exp(-x))


def _inproj_kernel(x_ref, g_ref, w_ref, o_ref, h_scr):
    @pl.when(pl.program_id(1) == 0)
    def _():
        x = x_ref[...]
        ms = jnp.mean(x * x, axis=-1, keepdims=True)
        h_scr[...] = (x * lax.rsqrt(ms + EPS) * g_ref[...]).astype(BF16)

    o_ref[...] = jnp.dot(h_scr[...], w_ref[...], preferred_element_type=F32).astype(o_ref.dtype)


def _in_proj(x, g, w, tm, tn):
    T, D = x.shape
    N = w.shape[1]
    return pl.pallas_call(
        _inproj_kernel,
        grid=(T // tm, N // tn),
        in_specs=[
            pl.BlockSpec((tm, D), lambda i, j: (i, 0)),
            pl.BlockSpec((1, D), lambda i, j: (0, 0)),
            pl.BlockSpec((D, tn), lambda i, j: (0, j)),
        ],
        out_specs=pl.BlockSpec((tm, tn), lambda i, j: (i, j)),
        out_shape=jax.ShapeDtypeStruct((T, N), BF16),
        scratch_shapes=[pltpu.VMEM((tm, D), BF16)],
        compiler_params=_params(("parallel", "arbitrary")),
        name="in_proj",
    )(x, g, w)


def _conv_kernel(val_ref, gate_ref, hval_ref, hgate_ref, gc_ref, dww_ref, dwb_ref,
                 lng_ref, lnb_ref, wpw_ref, o_ref, u_scr, y_scr):
    i = pl.program_id(0)
    ts, C = val_ref.shape
    halo = hval_ref.shape[0]
    ncb = C // LANES
    hu = hval_ref[...].astype(F32) * _sigmoid(hgate_ref[...].astype(F32))
    hu = jnp.where(i > 0, hu, 0.0)
    u = val_ref[...].astype(F32) * _sigmoid(gate_ref[...].astype(F32))
    for cb in range(ncb):
        lanes = slice(cb * LANES, (cb + 1) * LANES)
        u_scr[cb, 0:halo, :] = hu[:, lanes]
        u_scr[cb, halo:halo + ts, :] = u[:, lanes]

    off = halo - (CONV_WIDTH - 1)
    for cb in range(ncb):
        lanes = slice(cb * LANES, (cb + 1) * LANES)
        taps = [jnp.broadcast_to(dww_ref[j:j + 1, lanes], (8, LANES)) for j in range(CONV_WIDTH)]
        bias = jnp.broadcast_to(dwb_ref[:, lanes], (8, LANES))
        for rb in range(0, ts, CONV_ROWS):
            acc = [bias] * 8
            for s in range(8 + CONV_WIDTH - 1):
                xs = u_scr[cb, pl.ds(off + rb + s, 8, stride=8), :]
                for q in range(8):
                    if 0 <= s - q < CONV_WIDTH:
                        acc[q] = acc[q] + taps[s - q] * xs
            for q in range(8):
                y_scr[cb, pl.ds(rb + q, 8, stride=8), :] = acc[q]

    y = jnp.concatenate([y_scr[cb] for cb in range(ncb)], axis=1)
    mu = jnp.mean(y, axis=-1, keepdims=True)
    yc = y - mu
    var = jnp.mean(yc * yc, axis=-1, keepdims=True)
    z = yc * lax.rsqrt(var + EPS) * lng_ref[...] + lnb_ref[...]
    z = z * _sigmoid(z)
    out = jnp.dot(z.astype(BF16), wpw_ref[...], preferred_element_type=F32)
    o_ref[...] = (out * _sigmoid(gc_ref[...].astype(F32))).astype(o_ref.dtype)


def _conv_branch(proj, dww, dwb, lng, lnb, wpw, C, ts, gconv_col_block):
    T = proj.shape[0]
    assert ts % CONV_ROWS == 0 and C % LANES == 0
    hb = ts // CONV_HALO
    halo_map_v = lambda i: (jnp.maximum(i * hb - 1, 0), 0)
    halo_map_g = lambda i: (jnp.maximum(i * hb - 1, 0), 1)
    const = lambda i: (0, 0)
    return pl.pallas_call(
        _conv_kernel,
        grid=(T // ts,),
        in_specs=[
            pl.BlockSpec((ts, C), lambda i: (i, 0)),
            pl.BlockSpec((ts, C), lambda i: (i, 1)),
            pl.BlockSpec((CONV_HALO, C), halo_map_v),
            pl.BlockSpec((CONV_HALO, C), halo_map_g),
            pl.BlockSpec((ts, C), lambda i: (i, gconv_col_block)),
            pl.BlockSpec((CONV_WIDTH, C), const),
            pl.BlockSpec((1, C), const),
            pl.BlockSpec((1, C), const),
            pl.BlockSpec((1, C), const),
            pl.BlockSpec((C, C), const),
        ],
        out_specs=pl.BlockSpec((ts, C), lambda i: (i, 0)),
        out_shape=jax.ShapeDtypeStruct((T, C), BF16),
        scratch_shapes=[pltpu.VMEM((C // LANES, CONV_HALO + ts, LANES), F32),
                        pltpu.VMEM((C // LANES, ts, LANES), F32)],
        compiler_params=_params(("parallel",)),
        name="conv_branch",
    )(proj, proj, proj, proj, proj, dww, dwb, lng, lnb, wpw)


def _attn_kernel(q_ref, k_ref, *rest):
    v_refs = rest[:ATTN_KBLOCKS]
    bias_ref, qg_ref, kg_ref, o_ref, kn_scr = rest[ATTN_KBLOCKS:]
    i = pl.program_id(1)
    qb = q_ref.shape[0]
    nkb = ATTN_KBLOCKS

    def rms(t, g):
        t = t.astype(F32)
        return t * lax.rsqrt(jnp.mean(t * t, axis=-1, keepdims=True) + EPS) * g

    @pl.when(i == 0)
    def _():
        for b in range(1, nkb):
            kn_scr[b] = jnp.zeros(kn_scr.shape[1:], kn_scr.dtype)

    col = lax.broadcasted_iota(jnp.int32, (1, nkb * qb), 1)
    in_seq = col >= (nkb - 1 - i) * qb
    for hh in range(ATTN_HEADS_PER_STEP):
        lanes = slice(hh * HEAD_DIM, (hh + 1) * HEAD_DIM)
        kn_scr[i % nkb, :, lanes] = rms(k_ref[:, lanes], kg_ref[...]).astype(BF16)
    for hh in range(ATTN_HEADS_PER_STEP):
        lanes = slice(hh * HEAD_DIM, (hh + 1) * HEAD_DIM)
        q = (rms(q_ref[:, lanes], qg_ref[...]) * (HEAD_DIM ** -0.5 * LOG2E)).astype(BF16)
        k = jnp.concatenate([kn_scr[(i + 1 + b) % nkb, :, lanes] for b in range(nkb)], axis=0)
        v = jnp.concatenate([r[:, lanes] for r in v_refs], axis=0)
        s = lax.dot_general(q, k, (((1,), (1,)), ((), ())), preferred_element_type=F32)
        s = jnp.where(in_seq, s + bias_ref[hh], MASK_VALUE)
        m = jnp.max(s, axis=-1, keepdims=True)
        p = jnp.exp2(s - m)
        l = jnp.sum(p, axis=-1, keepdims=True)
        o = jnp.dot(p.astype(BF16), v, preferred_element_type=F32)
        o_ref[:, lanes] = (o / l).astype(o_ref.dtype)


def _attn_bias_table(rel_bias):
    qb, kb = ATTN_QBLOCK, ATTN_KBLOCKS * ATTN_QBLOCK
    n = 1 << int(np.ceil(np.log2(qb + kb)))
    dist = (kb - qb) - (np.arange(n) - (qb - 1))
    f = rel_bias[:, np.clip(dist, -(CHUNK - 1), MAX_REL) + (CHUNK - 1)].astype(F32)
    flat = jnp.tile(f, (1, qb))[:, :qb * (n - 1)]
    bias = flat.reshape(-1, qb, n - 1)[:, :, qb - 1:qb - 1 + kb]
    qpos = np.arange(qb)[:, None]
    kpos = np.arange(kb)[None, :] - (kb - qb)
    qc = qpos // CHUNK
    kc = np.floor_divide(kpos, CHUNK)
    valid = (kc <= qc) & (kc >= qc - N_LEFT_CHUNKS)
    return jnp.where(valid[None], bias * LOG2E, MASK_VALUE)


def _attention(proj, bias, qg, kg, q_col0, k_col0, v_col0):
    T = proj.shape[0]
    qb = ATTN_QBLOCK
    nkb = ATTN_KBLOCKS
    hp = ATTN_HEADS_PER_STEP
    w = hp * HEAD_DIM
    assert N_HEADS % hp == 0 and q_col0 % hp == 0 and k_col0 % hp == 0 and v_col0 % hp == 0

    def kv_spec(col0, back):
        return pl.BlockSpec((qb, w), lambda g, i: (jnp.maximum(i - back, 0), col0 // hp + g))

    k_specs = [kv_spec(k_col0, 0)]
    v_specs = [kv_spec(v_col0, nkb - 1 - b) for b in range(nkb)]
    return pl.pallas_call(
        _attn_kernel,
        grid=(N_HEADS // hp, T // qb),
        in_specs=[pl.BlockSpec((qb, w), lambda g, i: (i, q_col0 // hp + g))] + k_specs + v_specs + [
            pl.BlockSpec((hp, qb, nkb * qb), lambda g, i: (g, 0, 0)),
            pl.BlockSpec((1, HEAD_DIM), lambda g, i: (0, 0)),
            pl.BlockSpec((1, HEAD_DIM), lambda g, i: (0, 0)),
        ],
        out_specs=pl.BlockSpec((qb, w), lambda g, i: (i, g)),
        out_shape=jax.ShapeDtypeStruct((T, N_HEADS * HEAD_DIM), BF16),
        scratch_shapes=[pltpu.VMEM((nkb, qb, w), BF16)],
        compiler_params=_params(("parallel", "arbitrary")),
        name="chunk_attention",
    )(*([proj] * (2 + nkb)), bias, qg, kg)


def _merge_kernel(conv_ref, attn_ref, ga_ref, x_ref, wao_ref, wout_ref, g2_ref, wq_ref,
                  x1_ref, h2t_ref, qp_ref):
    a = jnp.dot(attn_ref[...], wao_ref[...], preferred_element_type=F32)
    merged = conv_ref[...].astype(F32) + _sigmoid(ga_ref[...].astype(F32)) * a
    x1 = x_ref[...] + jnp.dot(merged.astype(BF16), wout_ref[...], preferred_element_type=F32)
    x1_ref[...] = x1
    ms = jnp.mean(x1 * x1, axis=-1, keepdims=True)
    h2 = x1 * lax.rsqrt(ms + EPS) * g2_ref[...]
    h2t_ref[...] = h2.T.astype(BF16)
    qp_ref[...] = jnp.dot(h2.astype(BF16), wq_ref[...],
                          preferred_element_type=F32).astype(qp_ref.dtype)


def _merge(convg, attn, proj, x, wao, wout, g2, wq, tm, gattn_col_block):
    T, D = x.shape
    row = lambda i: (i, 0)
    const = lambda i: (0, 0)
    resident = functools.partial(pl.BlockSpec, pipeline_mode=pl.Buffered(1))
    return pl.pallas_call(
        _merge_kernel,
        grid=(T // tm,),
        in_specs=[
            pl.BlockSpec((tm, D), row),
            pl.BlockSpec((tm, D), row),
            pl.BlockSpec((tm, D), lambda i: (i, gattn_col_block)),
            pl.BlockSpec((tm, D), row),
            resident(wao.shape, const),
            resident(wout.shape, const),
            pl.BlockSpec((1, D), const),
            resident(wq.shape, const),
        ],
        out_specs=[pl.BlockSpec((tm, D), row), pl.BlockSpec((D, tm), lambda i: (0, i)),
                   pl.BlockSpec((tm, wq.shape[1]), row)],
        out_shape=[jax.ShapeDtypeStruct((T, D), F32), jax.ShapeDtypeStruct((D, T), BF16),
                   jax.ShapeDtypeStruct((T, wq.shape[1]), BF16)],
        compiler_params=_params(("parallel",)),
        name="merge_outproj",
    )(convg, attn, proj, x, wao, wout, g2, wq)


def _sort_network(n):
    pairs = []
    p = 1
    while p < n:
        k = p
        while k >= 1:
            for j in range(k % p, n - k, 2 * k):
                for i in range(min(k, n - j - k)):
                    if (i + j) // (2 * p) == (i + j + k) // (2 * p):
                        pairs.append((i + j, i + j + k))
            k //= 2
        p *= 2
    return pairs


def _compare_exchange(v, i, j):
    a, b = v[i], v[j]
    if a is None:
        v[i], v[j] = b, None
    elif b is not None:
        v[i], v[j] = jnp.maximum(a, b), jnp.minimum(a, b)


def _sort_desc(v):
    v = list(v)
    for i, j in _sort_network(len(v)):
        _compare_exchange(v, i, j)
    return v


def _bitonic_merge_desc(v):
    v = list(v)
    k = len(v) // 2
    while k >= 1:
        for i in range(len(v)):
            if i & k == 0:
                _compare_exchange(v, i, i + k)
        k //= 2
    return v


def _top_across_sublanes(v):
    n = len(v)
    for shift in (4, 2, 1):
        other = [None if x is None else pltpu.roll(x, shift, axis=0) for x in v]
        merged = []
        for i in range(n):
            a, b = v[i], other[n - 1 - i]
            merged.append(b if a is None else a if b is None else jnp.maximum(a, b))
        v = _bitonic_merge_desc(merged)
    return v


def _sublane_pack(vals, sub):
    out = vals[0]
    for k in range(1, 8):
        out = jnp.where(sub == k, vals[k], out)
    return out


def _route_chunk(s1, s2):
    n = PEER_TOPK
    v1 = _top_across_sublanes(_sort_desc(s1))
    v2 = _top_across_sublanes(_sort_desc(s2))
    sub = lax.broadcasted_iota(jnp.int32, v1[0].shape, 0)
    v2_lo, v2_hi = _sublane_pack(v2[:8], sub), _sublane_pack(v2[8:], sub)
    v1_hi = _sublane_pack(v1[8:], sub)
    cand = [v1[0] + v2_lo, v1[0] + v2_hi] + [v1[i] + v2_lo for i in range(1, 8)] + [v1_hi + v2[0]]
    cand = _top_across_sublanes(_sort_desc(cand + [None] * (n - len(cand))))
    top, tau = cand[0], cand[n - 1]
    z = None
    for c in cand:
        e = jnp.exp(c - top)
        z = e if z is None else z + e
    cutv = []
    for i in range(n):
        cv = jnp.full_like(top, jnp.inf)
        for j in range(n // (i + 1)):
            cv = jnp.where(v1[i] + v2[j] >= tau, v2[j], cv)
        cutv.append(cv)
    cuts = []
    for g in s1:
        cut = jnp.full_like(g, jnp.inf)
        for i in reversed(range(n)):
            cut = jnp.where(g >= v1[i], cutv[i], cut)
        cuts.append(cut)
    return cuts, v1[0], v2[0], 1.0 / z


def _route_kernel(q_ref, k1_ref, k2_ref, cut_ref, r1_ref, s2_ref, e2_ref):
    nt = (((1,), (1,)), ((), ()))
    groups = PEER_KEYS // 8
    for h in range(PEER_HEADS):
        base = h * 2 * PEER_HALF
        s1 = lax.dot_general(k1_ref[...], q_ref[:, base:base + PEER_HALF], nt,
                             preferred_element_type=F32)
        s2 = lax.dot_general(k2_ref[...], q_ref[:, base + PEER_HALF:base + 2 * PEER_HALF], nt,
                             preferred_element_type=F32)
        for c in range(cut_ref.shape[1]):
            lanes = slice(c * PEER_LANES, (c + 1) * PEER_LANES)
            g1 = [s1[8 * g:8 * g + 8, lanes] for g in range(groups)]
            g2 = [s2[8 * g:8 * g + 8, lanes] for g in range(groups)]
            cuts, m1, m2, rz = _route_chunk(g1, g2)
            for g in range(groups):
                rows = slice(8 * g, 8 * g + 8)
                cut_ref[h, c, rows, :] = cuts[g]
                r1_ref[h, c, rows, :] = jnp.exp(g1[g] - m1)
                s2_ref[h, c, rows, :] = g2[g]
                e2_ref[h, c, rows, :] = jnp.exp(g2[g] - m2) * rz


def _route(qp, k1, k2, tr):
    T = qp.shape[0]
    out = jax.ShapeDtypeStruct((PEER_HEADS, T // PEER_LANES, PEER_KEYS, PEER_LANES), F32)
    ospec = pl.BlockSpec((PEER_HEADS, tr // PEER_LANES, PEER_KEYS, PEER_LANES), lambda i: (0, i, 0, 0))
    return pl.pallas_call(
        _route_kernel,
        grid=(T // tr,),
        in_specs=[
            pl.BlockSpec((tr, qp.shape[1]), lambda i: (i, 0)),
            pl.BlockSpec(k1.shape, lambda i: (0, 0)),
            pl.BlockSpec(k2.shape, lambda i: (0, 0)),
        ],
        out_specs=[ospec] * 4,
        out_shape=[out] * 4,
        compiler_params=_params(("parallel",)),
        name="peer_route",
    )(qp, k1, k2)


def _gelu(a):
    return 0.5 * a * (1.0 + lax.erf(a * (2.0 ** -0.5)))


def _gate_tile(cut_ref, r1_ref, s2_ref, e2_ref, w_ref, tile, n_e1):
    e1_rows = pl.ds(pl.multiple_of(tile * n_e1, n_e1), n_e1)
    for c in range(cut_ref.shape[1]):
        lanes = slice(c * PEER_LANES, (c + 1) * PEER_LANES)
        cuts = [cut_ref[h, c, e1_rows, :] for h in range(PEER_HEADS)]
        r1s = [r1_ref[h, c, e1_rows, :] for h in range(PEER_HEADS)]
        for r in range(n_e1):
            w = None
            for h in range(PEER_HEADS):
                t = jnp.where(s2_ref[h, c] >= cuts[h][r:r + 1], e2_ref[h, c], 0.0)
                t = t * r1s[h][r:r + 1]
                w = t if w is None else w + t
            w_ref[r * PEER_KEYS:(r + 1) * PEER_KEYS, lanes] = w


def _peer_kernel(h2t_ref, u_ref, vt_ref, cut_ref, r1_ref, s2_ref, e2_ref, x1_ref, o_ref,
                 y_scr, a_scr, w_scr, p_scr):
    j = pl.program_id(1)
    nj = pl.num_programs(1)
    te = u_ref.shape[0]
    n_e1 = te // PEER_KEYS
    gate = functools.partial(_gate_tile, cut_ref, r1_ref, s2_ref, e2_ref, n_e1=n_e1)

    @pl.when(j == 0)
    def _():
        y_scr[...] = jnp.zeros_like(y_scr)
        gate(w_scr.at[0], 0)

    a_scr[...] = jnp.dot(u_ref[...], h2t_ref[...], preferred_element_type=F32)
    gate(w_scr.at[(j + 1) % 2], jnp.minimum(j + 1, nj - 1))
    p_scr[...] = (_gelu(a_scr[...]) * w_scr[j % 2]).astype(BF16)
    y_scr[...] += jnp.dot(vt_ref[...], p_scr[...], preferred_element_type=F32)

    @pl.when(j == pl.num_programs(1) - 1)
    def _():
        o_ref[...] = x1_ref[...] + y_scr[...].T


def _peer(h2t, u, vt, cut, r1, s2, e2, x1, tm, te):
    D, T = h2t.shape
    E = u.shape[0]
    assert te % (8 * PEER_KEYS) == 0 and tm % PEER_LANES == 0
    once = functools.partial(pl.BlockSpec, pipeline_mode=pl.Buffered(1))
    rspec = once((PEER_HEADS, tm // PEER_LANES, PEER_KEYS, PEER_LANES), lambda i, j: (0, i, 0, 0))
    return pl.pallas_call(
        _peer_kernel,
        grid=(T // tm, E // te),
        in_specs=[
            once((D, tm), lambda i, j: (0, i)),
            pl.BlockSpec((te, D), lambda i, j: (j, 0)),
            pl.BlockSpec((D, te), lambda i, j: (0, j)),
            rspec, rspec, rspec, rspec,
            once((tm, D), lambda i, j: (i, 0)),
        ],
        out_specs=pl.BlockSpec((tm, D), lambda i, j: (i, 0)),
        out_shape=jax.ShapeDtypeStruct((T, D), F32),
        scratch_shapes=[pltpu.VMEM((D, tm), F32), pltpu.VMEM((te, tm), F32),
                        pltpu.VMEM((2, te, tm), F32), pltpu.VMEM((te, tm), BF16)],
        compiler_params=_params(("parallel", "arbitrary")),
        name="peer_experts",
    )(h2t, u, vt, cut, r1, s2, e2, x1)


def _transpose_cast_kernel(v_ref, o_ref):
    o_ref[...] = v_ref[...].T.astype(o_ref.dtype)


def _transpose_cast(v, rows):
    E, D = v.shape
    return pl.pallas_call(
        _transpose_cast_kernel,
        grid=(E // rows,),
        in_specs=[pl.BlockSpec((rows, D), lambda i: (i, 0))],
        out_specs=pl.BlockSpec((D, rows), lambda i: (0, i)),
        out_shape=jax.ShapeDtypeStruct((D, E), BF16),
        compiler_params=_params(("parallel",)),
        name="expert_v_transpose",
    )(v)


def kernel(x, norm1_g, w_in, conv_dw_w, conv_dw_b, conv_ln_g, conv_ln_b, w_conv_out, q_norm_g,
           k_norm_g, rel_bias, w_attn_o, w_out, norm2_g, w_query, sub_keys_1, sub_keys_2,
           expert_u, expert_v):
    B, S, D = x.shape
    assert B == 1, "sequence-causal kernels are written for a single sequence"
    T = S
    C = conv_dw_w.shape[-1]
    A = N_HEADS * HEAD_DIM
    assert C == D and A == D and T % 512 == 0
    xt = x.reshape(T, D)
    for l in range(norm1_g.shape[0]):
        row = lambda v: v[l].reshape(1, -1).astype(F32)
        proj = _in_proj(xt, row(norm1_g), w_in[l].astype(BF16), tm=min(1024, T), tn=1024)
        convg = _conv_branch(proj, conv_dw_w[l], row(conv_dw_b), row(conv_ln_g), row(conv_ln_b),
                             w_conv_out[l].astype(BF16), C, ts=256,
                             gconv_col_block=(2 * C + 3 * A) // C)
        attn = _attention(proj, _attn_bias_table(rel_bias[l]), row(q_norm_g), row(k_norm_g),
                          q_col0=2 * C // HEAD_DIM, k_col0=(2 * C + A) // HEAD_DIM,
                          v_col0=(2 * C + 2 * A) // HEAD_DIM)
        x1, h2t, qp = _merge(convg, attn, proj, xt, w_attn_o[l].astype(BF16), w_out[l].astype(BF16),
                             row(norm2_g), w_query[l].astype(BF16), tm=256,
                             gattn_col_block=(2 * C + 3 * A + D) // D)
        cut, r1, s2, e2 = _route(qp, sub_keys_1[l].astype(BF16), sub_keys_2[l].astype(BF16), tr=256)
        xt = _peer(h2t, expert_u[l].astype(BF16), _transpose_cast(expert_v[l], rows=1024),
                   cut, r1, s2, e2, x1, tm=512, te=1024)
    return xt.reshape(B, S, D)
```
